```python
import math, functools
import jax, jax.numpy as jnp
from jax import lax
import numpy as np

D_MODEL = 1024
BATCH = 4
SEQ = 4096
DEPTH = 1
DEC_BATCH = 128
DEC_SEQ = 8
PAST_LEN = 16384
PAGE_SIZE = 128

MLA_HEADS = 16
QK_NOPE = 64
QK_ROPE = 32
V_HEAD = 64
Q_LORA = 512
KV_LORA = 256
ROPE_THETA = 10000.0
ATTN_SCALE = (QK_NOPE + QK_ROPE) ** -0.5
Q_BLOCK = 128
SSM_INNER = 2 * D_MODEL
SSM_HEAD_DIM = 64
SSM_HEADS = SSM_INNER // SSM_HEAD_DIM
SSM_GROUPS = 4
SSM_STATE = 128
CONV_WIDTH = 4
SSM_CHUNK = 128
CONV_DIM = SSM_INNER + 2 * SSM_GROUPS * SSM_STATE
D_FF = 2816
FFN_RES = 0.5
N_SUB = 3
EPS = 1e-6
IN_SIZES = (Q_LORA, KV_LORA, QK_ROPE, SSM_INNER, CONV_DIM, SSM_HEADS, D_MODEL, D_MODEL)
IN_OFFSETS = tuple(int(v) for v in np.cumsum(IN_SIZES)[:-1])
D_IN = int(sum(IN_SIZES))

kernel_name = 'hybrid_mla_ssd_macaron_step'


def rms_norm(x, g):
    xf = x.astype(jnp.float32)
    y = xf * lax.rsqrt(jnp.mean(xf * xf, axis=-1, keepdims=True) + EPS)
    return (y * g.astype(jnp.float32)).astype(x.dtype)


def rope(x, pos):
    half = QK_ROPE // 2
    inv = ROPE_THETA ** (-jnp.arange(half, dtype=jnp.float32) / half)
    ang = pos.astype(jnp.float32)[:, None] * inv[None, :]
    shape = (1, pos.shape[0]) + (1,) * (x.ndim - 3) + (half,)
    cos, sin = jnp.cos(ang).reshape(shape), jnp.sin(ang).reshape(shape)
    xf = x.astype(jnp.float32)
    x1, x2 = xf[..., :half], xf[..., half:]
    return jnp.concatenate([x1 * cos - x2 * sin, x1 * sin + x2 * cos], axis=-1).astype(x.dtype)


def _modulate(x, g, shift, scale):
    return rms_norm(x, g) * (1 + scale[:, None, :]) + shift[:, None, :]


def _swiglu(h, w_gate, w_up, w_down):
    return (jax.nn.silu(h @ w_gate) * (h @ w_up)) @ w_down


def _causal_conv(u, w, b):
    out = lax.conv_general_dilated(u, w[:, None, :], window_strides=(1,), padding='VALID',
                                   dimension_numbers=('NWC', 'WIO', 'NWC'),
                                   feature_group_count=u.shape[-1])
    return out + b


def _mla_project(q_lat, kv_lat, k_pe_raw, pos, g_q_lat, w_uq, g_kv_lat):
    bsz, T = q_lat.shape[:2]
    q = (rms_norm(q_lat, g_q_lat) @ w_uq).reshape(bsz, T, MLA_HEADS, QK_NOPE + QK_ROPE)
    q_nope, q_pe = q[..., :QK_NOPE], rope(q[..., QK_NOPE:], pos)
    c_kv = rms_norm(kv_lat, g_kv_lat)
    k_pe = rope(k_pe_raw, pos)
    return q_nope, q_pe, c_kv, k_pe


def _mla_prompt_attention(q_nope, q_pe, c_kv, k_pe, w_ukv):
    bsz, T = q_nope.shape[:2]
    w = w_ukv.reshape(KV_LORA, MLA_HEADS, QK_NOPE + V_HEAD)
    kv = jnp.einsum('btc,chd->bthd', c_kv, w)
    k_nope, v = kv[..., :QK_NOPE], kv[..., QK_NOPE:]
    q = jnp.concatenate([q_nope, q_pe], axis=-1)
    k = jnp.concatenate([k_nope, jnp.broadcast_to(k_pe[:, :, None, :], (bsz, T, MLA_HEADS, QK_ROPE))], axis=-1)
    n_blocks = T // Q_BLOCK
    q_blocks = jnp.moveaxis(q.reshape(bsz, n_blocks, Q_BLOCK, MLA_HEADS, QK_NOPE + QK_ROPE), 1, 0)
    key_pos = jnp.arange(T)

    def one_block(args):
        q_blk, blk = args
        s = jnp.einsum('bqhd,bkhd->bhqk', q_blk, k).astype(jnp.float32) * ATTN_SCALE
        q_pos = blk * Q_BLOCK + jnp.arange(Q_BLOCK)
        s = jnp.where(key_pos[None, :] <= q_pos[:, None], s, -jnp.inf)
        p = jax.nn.softmax(s, axis=-1).astype(v.dtype)
        return jnp.einsum('bhqk,bkhd->bqhd', p, v)

    out = lax.map(one_block, (q_blocks, jnp.arange(n_blocks)))
    return jnp.moveaxis(out, 0, 1).reshape(bsz, T, MLA_HEADS * V_HEAD)


def _mla_sample_attention(q_nope, q_pe, c_kv, k_pe, w_ukv, past_c, past_pe):
    bsz, T = q_nope.shape[:2]
    w = w_ukv.reshape(KV_LORA, MLA_HEADS, QK_NOPE + V_HEAD)
    w_uk, w_uv = w[..., :QK_NOPE], w[..., QK_NOPE:]
    q_abs = jnp.einsum('bthd,chd->bthc', q_nope, w_uk)
    s_past = (jnp.einsum('bthc,bpc->bhtp', q_abs, past_c).astype(jnp.float32)
              + jnp.einsum('bthr,bpr->bhtp', q_pe, past_pe).astype(jnp.float32))
    s_new = (jnp.einsum('bthc,bsc->bhts', q_abs, c_kv).astype(jnp.float32)
             + jnp.einsum('bthr,bsr->bhts', q_pe, k_pe).astype(jnp.float32))
    s_new = jnp.where(jnp.tril(jnp.ones((T, T), dtype=bool)), s_new, -jnp.inf)
    p = jax.nn.softmax(jnp.concatenate([s_past, s_new], axis=-1) * ATTN_SCALE, axis=-1).astype(q_nope.dtype)
    n_past = past_c.shape[1]
    o_lat = (jnp.einsum('bhtp,bpc->bthc', p[..., :n_past], past_c)
             + jnp.einsum('bhts,bsc->bthc', p[..., n_past:], c_kv))
    o = jnp.einsum('bthc,chd->bthd', o_lat, w_uv)
    return o.reshape(bsz, T, MLA_HEADS * V_HEAD)


def _ssd_chunked_scan(xs, dt, a, b_in, c_in, h0):
    bsz, T = xs.shape[:2]
    L = min(SSM_CHUNK, T)
    pad = (-T) % L
    R = SSM_HEADS // SSM_GROUPS
    xdt = xs.astype(jnp.float32) * dt[..., None]
    la = dt * a
    bf, cf = b_in.astype(jnp.float32), c_in.astype(jnp.float32)
    if pad:
        pad_t = lambda t: jnp.pad(t, [(0, 0), (0, pad)] + [(0, 0)] * (t.ndim - 2))
        xdt, la, bf, cf = pad_t(xdt), pad_t(la), pad_t(bf), pad_t(cf)
    nc = (T + pad) // L
    xdt = xdt.reshape(bsz, nc, L, SSM_GROUPS, R, SSM_HEAD_DIM)
    la = la.reshape(bsz, nc, L, SSM_GROUPS, R)
    bf = bf.reshape(bsz, nc, L, SSM_GROUPS, SSM_STATE)
    cf = cf.reshape(bsz, nc, L, SSM_GROUPS, SSM_STATE)
    cum = jnp.cumsum(la, axis=2)
    causal = jnp.tril(jnp.ones((L, L), dtype=bool))[:, :, None, None]
    seg = cum[:, :, :, None] - cum[:, :, None, :]
    decay = jnp.exp(jnp.where(causal, seg, -jnp.inf))
    cb = jnp.einsum('bclgn,bcsgn->bclsg', cf, bf)
    y_diag = jnp.einsum('bclsgr,bcsgrp->bclgrp', cb[..., None] * decay, xdt)
    to_end = jnp.exp(cum[:, :, -1:] - cum)
    chunk_states = jnp.einsum('bclgn,bclgr,bclgrp->bcgrpn', bf, to_end, xdt)
    chunk_decay = jnp.exp(cum[:, :, -1])

    def step(h, inp):
        s_c, d_c = inp
        return h * d_c[..., None, None] + s_c, h

    h_init = h0.astype(jnp.float32).reshape(bsz, SSM_GROUPS, R, SSM_HEAD_DIM, SSM_STATE)
    h_final, h_enter = lax.scan(step, h_init, (jnp.moveaxis(chunk_states, 1, 0), jnp.moveaxis(chunk_decay, 1, 0)))
    h_enter = jnp.moveaxis(h_enter, 0, 1)
    y_off = jnp.einsum('bclgn,bcgrpn,bclgr->bclgrp', cf, h_enter, jnp.exp(cum))
    y = (y_diag + y_off).reshape(bsz, nc * L, SSM_HEADS, SSM_HEAD_DIM)[:, :T]
    return y, h_final.reshape(bsz, SSM_HEADS, SSM_HEAD_DIM, SSM_STATE)


def _ssd_branch(z, xbc_raw, dt_raw, conv_prev, ssm_prev, conv_w, conv_b, dt_bias, a_log, d_skip, g_ssm_norm, w_o_ssm):
    bsz, T = z.shape[:2]
    gn = SSM_GROUPS * SSM_STATE
    u = jnp.concatenate([conv_prev, xbc_raw], axis=1)
    new_conv = u[:, u.shape[1] - (CONV_WIDTH - 1):]
    xbc = jax.nn.silu(_causal_conv(u, conv_w, conv_b))
    xs = xbc[..., :SSM_INNER].reshape(bsz, T, SSM_HEADS, SSM_HEAD_DIM)
    b_in = xbc[..., SSM_INNER:SSM_INNER + gn].reshape(bsz, T, SSM_GROUPS, SSM_STATE)
    c_in = xbc[..., SSM_INNER + gn:].reshape(bsz, T, SSM_GROUPS, SSM_STATE)
    dt = jax.nn.softplus(dt_raw.astype(jnp.float32) + dt_bias.astype(jnp.float32))
    a = -jnp.exp(a_log.astype(jnp.float32))
    y, h_final = _ssd_chunked_scan(xs, dt, a, b_in, c_in, ssm_prev)
    y = (y + d_skip.astype(jnp.float32)[:, None] * xs.astype(jnp.float32)).astype(z.dtype)
    y = rms_norm(y.reshape(bsz, T, SSM_INNER) * jax.nn.silu(z), g_ssm_norm)
    return y @ w_o_ssm, new_conv, h_final.astype(ssm_prev.dtype)


def _layer(x, c, pos, attend, conv_prev, ssm_prev, w_ada, b_ada, g_pre, g_post, w_ffn_gate, w_ffn_up,
           w_ffn_down, w_in, g_q_lat, w_uq, g_kv_lat, w_ukv, w_o_attn, conv_w, conv_b, dt_bias, a_log,
           d_skip, g_ssm_norm, w_o_ssm, w_out):
    bsz = x.shape[0]
    mod = (jax.nn.silu(c) @ w_ada + b_ada).reshape(bsz, N_SUB, 3, D_MODEL)
    shift, scale, gate = mod[:, :, 0], mod[:, :, 1], mod[:, :, 2]
    h = _modulate(x, g_pre[0], shift[:, 0], scale[:, 0])
    f = _swiglu(h, w_ffn_gate[0], w_ffn_up[0], w_ffn_down[0])
    x = x + FFN_RES * gate[:, 0][:, None] * rms_norm(f, g_post[0])
    h = _modulate(x, g_pre[1], shift[:, 1], scale[:, 1])
    q_lat, kv_lat, k_pe_raw, z, xbc_raw, dt_raw, g_attn, g_ssm = jnp.split(h @ w_in, IN_OFFSETS, axis=-1)
    q_nope, q_pe, c_kv, k_pe = _mla_project(q_lat, kv_lat, k_pe_raw, pos, g_q_lat, w_uq, g_kv_lat)
    o_attn = attend(q_nope, q_pe, c_kv, k_pe, w_ukv) @ w_o_attn
    o_ssm, new_conv, new_ssm = _ssd_branch(z, xbc_raw, dt_raw, conv_prev, ssm_prev, conv_w, conv_b,
                                           dt_bias, a_log, d_skip, g_ssm_norm, w_o_ssm)
    mixed = (jax.nn.sigmoid(g_attn) * o_attn + jax.nn.sigmoid(g_ssm) * o_ssm) @ w_out
    x = x + gate[:, 1][:, None] * rms_norm(mixed, g_post[1])
    h = _modulate(x, g_pre[2], shift[:, 2], scale[:, 2])
    f = _swiglu(h, w_ffn_gate[1], w_ffn_up[1], w_ffn_down[1])
    x = x + FFN_RES * gate[:, 2][:, None] * rms_norm(f, g_post[2])
    return x, c_kv, k_pe, new_conv, new_ssm


def _normal(k, shape, scale):
    return jax.random.normal(k, shape, jnp.float32) * scale


def setup_inputs(seed: int = 0) -> dict:
    key = jax.random.key(seed)
    ks = jax.random.split(key, 32)
    n_pages = PAST_LEN // PAGE_SIZE
    n_used = DEC_BATCH * n_pages
    n_phys = (5 * n_used + 3) // 4
    page_table = jax.random.permutation(ks[0], n_phys)[:n_used].reshape(DEC_BATCH, n_pages).astype(jnp.int32)
    dt0 = jnp.exp(jax.random.uniform(ks[25], (DEPTH, SSM_HEADS), jnp.float32, math.log(1e-3), math.log(1e-1)))
    return {
        'x_prompt': _normal(ks[1], (BATCH, SEQ, D_MODEL), 1.0),
        'x_sample': _normal(ks[2], (DEC_BATCH, DEC_SEQ, D_MODEL), 1.0),
        'cache_kv_latent': _normal(ks[3], (DEPTH, n_phys, PAGE_SIZE, KV_LORA), 1.0),
        'cache_k_rope': _normal(ks[4], (DEPTH, n_phys, PAGE_SIZE, QK_ROPE), 1.0),
        'state_conv': _normal(ks[5], (DEPTH, DEC_BATCH, CONV_WIDTH - 1, CONV_DIM), 1.0),
        'state_ssm': _normal(ks[6], (DEPTH, DEC_BATCH, SSM_HEADS, SSM_HEAD_DIM, SSM_STATE), 0.1),
        'page_table': page_table,
        'c_prompt': _normal(ks[7], (BATCH, D_MODEL), 1.0),
        'c_sample': _normal(ks[8], (DEC_BATCH, D_MODEL), 1.0),
        'w_ada': _normal(ks[9], (DEPTH, D_MODEL, N_SUB * 3 * D_MODEL), 0.5 * D_MODEL ** -0.5),
        'b_ada': _normal(ks[10], (DEPTH, N_SUB * 3 * D_MODEL), 0.01),
        'g_pre': 1.0 + _normal(ks[11], (DEPTH, N_SUB, D_MODEL), 0.01),
        'g_post': 1.0 + _normal(ks[12], (DEPTH, N_SUB, D_MODEL), 0.01),
        'w_ffn_gate': _normal(ks[13], (DEPTH, 2, D_MODEL, D_FF), D_MODEL ** -0.5),
        'w_ffn_up': _normal(ks[14], (DEPTH, 2, D_MODEL, D_FF), D_MODEL ** -0.5),
        'w_ffn_down': _normal(ks[15], (DEPTH, 2, D_FF, D_MODEL), D_FF ** -0.5),
        'w_in': _normal(ks[16], (DEPTH, D_MODEL, D_IN), D_MODEL ** -0.5),
        'g_q_lat': 1.0 + _normal(ks[17], (DEPTH, Q_LORA), 0.01),
        'w_uq': _normal(ks[18], (DEPTH, Q_LORA, MLA_HEADS * (QK_NOPE + QK_ROPE)), Q_LORA ** -0.5),
        'g_kv_lat': 1.0 + _normal(ks[19], (DEPTH, KV_LORA), 0.01),
        'w_ukv': _normal(ks[20], (DEPTH, KV_LORA, MLA_HEADS * (QK_NOPE + V_HEAD)), KV_LORA ** -0.5),
        'w_o_attn': _normal(ks[21], (DEPTH, MLA_HEADS * V_HEAD, D_MODEL), (MLA_HEADS * V_HEAD) ** -0.5),
        'conv_w': _normal(ks[22], (DEPTH, CONV_WIDTH, CONV_DIM), CONV_WIDTH ** -0.5),
        'conv_b': _normal(ks[23], (DEPTH, CONV_DIM), 0.01),
        'dt_bias': dt0 + jnp.log(-jnp.expm1(-dt0)),
        'a_log': jnp.log(jax.random.uniform(ks[24], (DEPTH, SSM_HEADS), jnp.float32, 1.0, 16.0)),
        'd_skip': 1.0 + _normal(ks[26], (DEPTH, SSM_HEADS), 0.01),
        'g_ssm_norm': 1.0 + _normal(ks[27], (DEPTH, SSM_INNER), 0.01),
        'w_o_ssm': _normal(ks[28], (DEPTH, SSM_INNER, D_MODEL), SSM_INNER ** -0.5),
        'w_out': _normal(ks[29], (DEPTH, D_MODEL, D_MODEL), D_MODEL ** -0.5),
    }


def reference(x_prompt, x_sample, cache_kv_latent, cache_k_rope, state_conv, state_ssm, page_table,
              c_prompt, c_sample, w_ada, b_ada, g_pre, g_post, w_ffn_gate, w_ffn_up, w_ffn_down, w_in,
              g_q_lat, w_uq, g_kv_lat, w_ukv, w_o_attn, conv_w, conv_b, dt_bias, a_log, d_skip,
              g_ssm_norm, w_o_ssm, w_out):
    bp, t_prompt = x_prompt.shape[:2]
    bs, t_sample = x_sample.shape[:2]
    past_len = page_table.shape[1] * cache_kv_latent.shape[2]
    pos_p = jnp.arange(t_prompt, dtype=jnp.int32)
    pos_s = past_len + jnp.arange(t_sample, dtype=jnp.int32)
    yp, ys = x_prompt, x_sample
    kv_p, pe_p, cv_p, ss_p, kv_s, pe_s, cv_s, ss_s = [], [], [], [], [], [], [], []
    for l in range(DEPTH):
        lw = (w_ada[l], b_ada[l], g_pre[l], g_post[l], w_ffn_gate[l], w_ffn_up[l], w_ffn_down[l], w_in[l],
              g_q_lat[l], w_uq[l], g_kv_lat[l], w_ukv[l], w_o_attn[l], conv_w[l], conv_b[l], dt_bias[l],
              a_log[l], d_skip[l], g_ssm_norm[l], w_o_ssm[l], w_out[l])
        conv0 = jnp.zeros((bp, CONV_WIDTH - 1, CONV_DIM), x_prompt.dtype)
        ssm0 = jnp.zeros((bp, SSM_HEADS, SSM_HEAD_DIM, SSM_STATE), state_ssm.dtype)
        yp, a1, a2, a3, a4 = _layer(yp, c_prompt, pos_p, _mla_prompt_attention, conv0, ssm0, *lw)
        past_c = cache_kv_latent[l][page_table].reshape(bs, past_len, KV_LORA)
        past_pe = cache_k_rope[l][page_table].reshape(bs, past_len, QK_ROPE)
        attend_s = functools.partial(_mla_sample_attention, past_c=past_c, past_pe=past_pe)
        ys, b1, b2, b3, b4 = _layer(ys, c_sample, pos_s, attend_s, state_conv[l], state_ssm[l], *lw)
        kv_p.append(a1); pe_p.append(a2); cv_p.append(a3); ss_p.append(a4)
        kv_s.append(b1); pe_s.append(b2); cv_s.append(b3); ss_s.append(b4)
    return (yp, ys, jnp.stack(kv_p), jnp.stack(pe_p), jnp.stack(cv_p), jnp.stack(ss_p),
            jnp.stack(kv_s), jnp.stack(pe_s), jnp.stack(cv_s), jnp.stack(ss_s))
```

```python
import functools

import jax
import jax.numpy as jnp
import numpy as np
from jax import lax
from jax.experimental import pallas as pl
from jax.experimental.pallas import tpu as pltpu

D_MODEL = 1024
MLA_HEADS = 16
QK_NOPE = 64
QK_ROPE = 32
V_HEAD = 64
Q_LORA = 512
KV_LORA = 256
ROPE_THETA = 10000.0
ATTN_SCALE = (QK_NOPE + QK_ROPE) ** -0.5
SSM_INNER = 2 * D_MODEL
SSM_HEAD_DIM = 64
SSM_HEADS = SSM_INNER // SSM_HEAD_DIM
SSM_GROUPS = 4
SSM_STATE = 128
CONV_WIDTH = 4
SSM_CHUNK = 128
CONV_DIM = SSM_INNER + 2 * SSM_GROUPS * SSM_STATE
D_FF = 2816
FFN_RES = 0.5
N_SUB = 3
EPS = 1e-6
IN_SIZES = (Q_LORA, KV_LORA, QK_ROPE, SSM_INNER, CONV_DIM, SSM_HEADS, D_MODEL, D_MODEL)
IN_OFFSETS = tuple(int(v) for v in np.cumsum((0,) + IN_SIZES))

LANES = 128
SUBLANES = 8
VMEM_LIMIT_BYTES = 56 * 1024 * 1024

HEAD_PAD = LANES
HEADS_PER_GROUP = SSM_HEADS // SSM_GROUPS
GROUP_CH = HEADS_PER_GROUP * SSM_HEAD_DIM
SMALL_W = Q_LORA + KV_LORA + 2 * LANES
KROPE_CHUNK = (Q_LORA + KV_LORA) // LANES
DT_CHUNK = KROPE_CHUNK + 1

BF16 = jnp.bfloat16
F32 = jnp.float32
_NT = (((1,), (1,)), ((), ()))


def _tile(n, target):
    t = min(n, target)
    while n % t:
        t -= 1
    return t


def _params(*sem):
    return pltpu.CompilerParams(dimension_semantics=sem, vmem_limit_bytes=VMEM_LIMIT_BYTES)


def _rms(x, g):
    return x * lax.rsqrt(jnp.mean(x * x, axis=-1, keepdims=True) + EPS) * g


def _silu(x):
    return x * jax.nn.sigmoid(x)


def _dot(a, b):
    return jnp.dot(a, b, preferred_element_type=F32)


def _split2_dot(v, m):
    hi = v.astype(BF16)
    mid = (v - hi.astype(F32)).astype(BF16)
    return _dot(hi, m) + _dot(mid, m)


def _split3_dot(m, v):
    hi = v.astype(BF16)
    r1 = v - hi.astype(F32)
    mid = r1.astype(BF16)
    lo = (r1 - mid.astype(F32)).astype(BF16)
    return _dot(m, hi) + _dot(m, mid) + _dot(m, lo)


def _mod_spec(mod, tm, rows_per_seq, ngrid):
    d = mod.shape[-1]
    if mod.ndim == 3:
        tiles = rows_per_seq // tm
        if ngrid == 1:
            return pl.BlockSpec((None, 1, d), lambda i: (i // tiles, 0, 0))
        return pl.BlockSpec((None, 1, d), lambda i, j: (i // tiles, 0, 0))
    if ngrid == 1:
        return pl.BlockSpec((tm, d), lambda i: (i, 0))
    return pl.BlockSpec((tm, d), lambda i, j: (i, 0))


def _row_spec(width, ngrid):
    if ngrid == 1:
        return pl.BlockSpec((1, width), lambda i: (0, 0))
    return pl.BlockSpec((1, width), lambda i, j: (0, 0))


def _ada_kernel(c_ref, w_ref, b_ref, o_ref):
    h = _silu(c_ref[...]).astype(BF16)
    o_ref[...] = _dot(h, w_ref[...].astype(BF16)) + b_ref[...]


def _ada(c, w, b):
    m, d = c.shape
    n = w.shape[1]
    tn = _tile(n, 1024)
    return pl.pallas_call(
        _ada_kernel,
        grid=(n // tn,),
        in_specs=[pl.BlockSpec((m, d), lambda j: (0, 0)),
                  pl.BlockSpec((d, tn), lambda j: (0, j)),
                  pl.BlockSpec((1, tn), lambda j: (0, j))],
        out_specs=pl.BlockSpec((m, tn), lambda j: (0, j)),
        out_shape=jax.ShapeDtypeStruct((m, n), F32),
        compiler_params=_params("arbitrary"),
    )(c, w, b.reshape(1, n))


def _ffn_kernel(x_ref, shift_ref, scale_ref, gate_ref, gpre_ref, gpost_ref, wg_ref, wu_ref, wd_ref,
                o_ref, h_scr, acc_scr):
    k = pl.program_id(1)

    @pl.when(k == 0)
    def _():
        h = _rms(x_ref[...], gpre_ref[...]) * (1.0 + scale_ref[...]) + shift_ref[...]
        h_scr[...] = h.astype(BF16)
        acc_scr[...] = jnp.zeros_like(acc_scr)

    h = h_scr[...]
    a = _dot(h, wg_ref[...])
    u = _dot(h, wu_ref[...])
    acc_scr[...] += _dot((_silu(a) * u).astype(BF16), wd_ref[...])

    @pl.when(k == pl.num_programs(1) - 1)
    def _():
        f = _rms(acc_scr[...], gpost_ref[...])
        o_ref[...] = x_ref[...] + FFN_RES * gate_ref[...] * f


def _ffn(x, shift, scale, gate, g_pre, g_post, wg, wu, wd, rows_per_seq):
    n, d = x.shape
    dff = wg.shape[1]
    tm = _tile(rows_per_seq if shift.ndim == 3 else n, 1024)
    tf = _tile(dff, 256)
    mspec = lambda m: _mod_spec(m, tm, rows_per_seq, 2)
    return pl.pallas_call(
        _ffn_kernel,
        grid=(n // tm, dff // tf),
        in_specs=[pl.BlockSpec((tm, d), lambda i, k: (i, 0)),
                  mspec(shift), mspec(scale), mspec(gate),
                  _row_spec(d, 2), _row_spec(d, 2),
                  pl.BlockSpec((d, tf), lambda i, k: (0, k)),
                  pl.BlockSpec((d, tf), lambda i, k: (0, k)),
                  pl.BlockSpec((tf, d), lambda i, k: (k, 0))],
        out_specs=pl.BlockSpec((tm, d), lambda i, k: (i, 0)),
        out_shape=jax.ShapeDtypeStruct((n, d), F32),
        scratch_shapes=[pltpu.VMEM((tm, d), BF16), pltpu.VMEM((tm, d), F32)],
        compiler_params=_params("parallel", "arbitrary"),
    )(x, shift, scale, gate, g_pre.reshape(1, d), g_post.reshape(1, d), wg, wu, wd)


def _proj_kernel(x_ref, shift_ref, scale_ref, gpre_ref, w_ref, o_ref, h_scr):
    @pl.when(pl.program_id(1) == 0)
    def _():
        h = _rms(x_ref[...], gpre_ref[...]) * (1.0 + scale_ref[...]) + shift_ref[...]
        h_scr[...] = h.astype(BF16)

    o_ref[...] = _dot(h_scr[...], w_ref[...]).astype(o_ref.dtype)


def _proj(x, shift, scale, g_pre, w, out_dtype, rows_per_seq):
    n, d = x.shape
    nout = w.shape[1]
    tm = _tile(rows_per_seq if shift.ndim == 3 else n, 1024)
    tn = _tile(nout, 1024)
    mspec = lambda m: _mod_spec(m, tm, rows_per_seq, 2)
    return pl.pallas_call(
        _proj_kernel,
        grid=(n // tm, nout // tn),
        in_specs=[pl.BlockSpec((tm, d), lambda i, j: (i, 0)),
                  mspec(shift), mspec(scale), _row_spec(d, 2),
                  pl.BlockSpec((d, tn), lambda i, j: (0, j))],
        out_specs=pl.BlockSpec((tm, tn), lambda i, j: (i, j)),
        out_shape=jax.ShapeDtypeStruct((n, nout), out_dtype),
        scratch_shapes=[pltpu.VMEM((tm, d), BF16)],
        compiler_params=_params("parallel", "arbitrary"),
    )(x, shift, scale, g_pre.reshape(1, d), w)


def _rope(x, cos, sin):
    w = x.shape[1]
    lane = lax.broadcasted_iota(jnp.int32, x.shape, 1) % LANES
    first_half = (lane >= QK_NOPE) & (lane < QK_NOPE + QK_ROPE // 2)
    swapped = jnp.where(first_half, pltpu.roll(x, w - QK_ROPE // 2, 1), pltpu.roll(x, QK_ROPE // 2, 1))
    return x * cos + swapped * sin


def _mla_kernel(prompt, small_ref, cos_ref, sin_ref, gq_ref, gkv_ref, wuq_ref, *rest):
    if prompt:
        wk_ref, wv_ref, q_ref, ckv_ref, kpe_ref, k_ref, v_ref = rest
    else:
        q_ref, ckv_ref, kpe_ref = rest
    sm = small_ref[...]
    cos, sin = cos_ref[...], sin_ref[...]
    qn = _rms(sm[:, :Q_LORA], gq_ref[...])
    q = _dot(qn.astype(BF16), wuq_ref[...])
    q_ref[...] = _rope(q, jnp.tile(cos, (1, MLA_HEADS)), jnp.tile(sin, (1, MLA_HEADS))).astype(BF16)
    ckv = _rms(sm[:, Q_LORA:Q_LORA + KV_LORA], gkv_ref[...])
    ckv_ref[...] = ckv
    kr = _rope(sm[:, KROPE_CHUNK * LANES:(KROPE_CHUNK + 1) * LANES], cos, sin)
    kpe_ref[...] = kr[:, QK_NOPE:QK_NOPE + QK_ROPE]
    if prompt:
        cb = ckv.astype(BF16)
        k_ref[...] = (_dot(cb, wk_ref[...]) + jnp.tile(kr, (1, MLA_HEADS))).astype(BF16)
        v_ref[...] = _dot(cb, wv_ref[...]).astype(BF16)


def _mla(small, cos, sin, g_q, g_kv, wuq, wk, wv, rows_per_seq, prompt):
    n = small.shape[0]
    hw = MLA_HEADS * HEAD_PAD
    tm = _tile(rows_per_seq if prompt else n, 512)
    tiles = rows_per_seq // tm if prompt else 1
    tab = pl.BlockSpec((tm, LANES), (lambda i: (i % tiles, 0)) if prompt else (lambda i: (0, 0)))
    const = lambda r, c: pl.BlockSpec((r, c), lambda i: (0, 0))
    rows = lambda c: pl.BlockSpec((tm, c), lambda i: (i, 0))
    in_specs = [rows(SMALL_W), tab, tab, const(1, Q_LORA), const(1, KV_LORA), const(Q_LORA, hw)]
    args = [small, cos, sin, g_q.reshape(1, -1), g_kv.reshape(1, -1), wuq]
    out_specs = [rows(hw), rows(KV_LORA), rows(QK_ROPE)]
    out_shape = [jax.ShapeDtypeStruct((n, hw), BF16), jax.ShapeDtypeStruct((n, KV_LORA), F32),
                 jax.ShapeDtypeStruct((n, QK_ROPE), F32)]
    if prompt:
        in_specs += [const(KV_LORA, hw), const(KV_LORA, hw)]
        args += [wk, wv]
        out_specs += [rows(hw), rows(hw)]
        out_shape += [jax.ShapeDtypeStruct((n, hw), BF16)] * 2
    return pl.pallas_call(
        functools.partial(_mla_kernel, prompt),
        grid=(n // tm,),
        in_specs=in_specs, out_specs=out_specs, out_shape=out_shape,
        compiler_params=_params("parallel"),
    )(*args)


def _attn_kernel(tq, tk, q_ref, k_ref, v_ref, o_ref):
    qi = pl.program_id(2)
    q = q_ref[...]
    q_pos = qi * tq + lax.broadcasted_iota(jnp.int32, (tq, tk), 0)
    k_off = lax.broadcasted_iota(jnp.int32, (tq, tk), 1)

    def body(kb, carry):
        m, l, acc = carry
        ks = pl.multiple_of(kb * tk, tk)
        k = k_ref[pl.ds(ks, tk), :]
        v = v_ref[pl.ds(ks, tk), :]
        s = lax.dot_general(q, k, _NT, preferred_element_type=F32) * ATTN_SCALE
        s = jnp.where(k_off + ks <= q_pos, s, -jnp.inf)
        m_new = jnp.maximum(m, jnp.max(s, axis=-1, keepdims=True))
        alpha = jnp.exp(m - m_new)
        p = jnp.exp(s - m_new)
        l = alpha * l + jnp.sum(p, axis=-1, keepdims=True)
        acc = alpha * acc + _dot(p.astype(BF16), v)
        return m_new, l, acc

    init = (jnp.full((tq, 1), -jnp.inf, F32), jnp.zeros((tq, 1), F32), jnp.zeros((tq, HEAD_PAD), F32))
    _, l, acc = lax.fori_loop(0, (qi + 1) * (tq // tk), body, init)
    o_ref[...] = (acc / l).astype(BF16)


def _prompt_attention(q, k, v, bsz, t):
    tq = _tile(t, 512)
    tk = _tile(tq, 512)
    nq = t // tq
    return pl.pallas_call(
        functools.partial(_attn_kernel, tq, tk),
        grid=(bsz, MLA_HEADS, nq),
        in_specs=[pl.BlockSpec((tq, HEAD_PAD), lambda b, h, i: (b * nq + i, h)),
                  pl.BlockSpec((t, HEAD_PAD), lambda b, h, i: (b, h)),
                  pl.BlockSpec((t, HEAD_PAD), lambda b, h, i: (b, h))],
        out_specs=pl.BlockSpec((tq, HEAD_PAD), lambda b, h, i: (b * nq + i, h)),
        out_shape=jax.ShapeDtypeStruct(q.shape, BF16),
        compiler_params=_params("parallel", "parallel", "arbitrary"),
    )(q, k, v)


def _head_mm_kernel(x_ref, w_ref, o_ref):
    o_ref[...] = _dot(x_ref[...], w_ref[...]).astype(o_ref.dtype)


def _head_mm(x, w):
    n = x.shape[0]
    nh, kin, kout = w.shape
    return pl.pallas_call(
        _head_mm_kernel,
        grid=(nh,),
        in_specs=[pl.BlockSpec((n, kin), lambda h: (0, h)),
                  pl.BlockSpec((None, kin, kout), lambda h: (h, 0, 0))],
        out_specs=pl.BlockSpec((n, kout), lambda h: (0, h)),
        out_shape=jax.ShapeDtypeStruct((n, nh * kout), BF16),
        compiler_params=_params("parallel"),
    )(x, w)


def _paged_kernel(pp, t_new, pt_ref, q_ref, cnew_ref, penew_ref, *rest):
    lat_refs, pe_refs = rest[:pp], rest[pp:2 * pp]
    o_ref, m_scr, l_scr, acc_scr = rest[2 * pp:]
    j = pl.program_id(1)
    rows = q_ref.shape[0]

    @pl.when(j == 0)
    def _():
        m_scr[...] = jnp.full_like(m_scr, -jnp.inf)
        l_scr[...] = jnp.zeros_like(l_scr)
        acc_scr[...] = jnp.zeros_like(acc_scr)

    q = q_ref[...]
    q_abs = q[:, :KV_LORA]
    q_pe = q[:, KV_LORA:KV_LORA + LANES]

    def scores(c, pe):
        pe = jnp.concatenate([pe, jnp.zeros((pe.shape[0], LANES - QK_ROPE), BF16)], axis=1)
        return (lax.dot_general(q_abs, c, _NT, preferred_element_type=F32)
                + lax.dot_general(q_pe, pe, _NT, preferred_element_type=F32))

    def update(s, cs):
        m = m_scr[...]
        m_new = jnp.maximum(m, jnp.max(s, axis=-1, keepdims=True))
        alpha = jnp.exp(m - m_new)
        p = jnp.exp(s - m_new)
        l_scr[...] = alpha * l_scr[...] + jnp.sum(p, axis=-1, keepdims=True)
        pb = p.astype(BF16)
        pv = _dot(pb[:, :cs[0].shape[0]], cs[0])
        for i in range(1, len(cs)):
            n0 = cs[0].shape[0]
            pv += _dot(pb[:, i * n0:(i + 1) * n0], cs[i])
        acc_scr[...] = alpha * acc_scr[...] + pv
        m_scr[...] = m_new

    cs = [r[...].astype(BF16) for r in lat_refs]
    s = jnp.concatenate([scores(c, r[...].astype(BF16)) for c, r in zip(cs, pe_refs)], axis=1)
    update(s * ATTN_SCALE, cs)

    @pl.when(j == pl.num_programs(1) - 1)
    def _():
        pad = LANES - t_new
        c = jnp.concatenate([cnew_ref[...], jnp.zeros((pad, KV_LORA), F32)], axis=0).astype(BF16)
        pe = jnp.concatenate([penew_ref[...], jnp.zeros((pad, QK_ROPE), F32)], axis=0).astype(BF16)
        s_new = scores(c, pe) * ATTN_SCALE
        t_row = lax.broadcasted_iota(jnp.int32, (rows, LANES), 0) // MLA_HEADS
        key = lax.broadcasted_iota(jnp.int32, (rows, LANES), 1)
        update(jnp.where(key <= t_row, s_new, -jnp.inf), [c])
        o_ref[...] = (acc_scr[...] / l_scr[...]).astype(BF16)


def _paged_attention(q, c_new, pe_new, cache_c, cache_pe, page_table):
    bsz, rows, qw = q.shape
    t_new = c_new.shape[1]
    n_pages = page_table.shape[1]
    page = cache_c.shape[1]
    pp = _tile(n_pages, 16)

    def page_map(i):
        return lambda b, j, pt: (pt[b * n_pages + j * pp + i], 0, 0)

    in_specs = [pl.BlockSpec((None, rows, qw), lambda b, j, pt: (b, 0, 0)),
                pl.BlockSpec((None, t_new, KV_LORA), lambda b, j, pt: (b, 0, 0)),
                pl.BlockSpec((None, t_new, QK_ROPE), lambda b, j, pt: (b, 0, 0))]
    in_specs += [pl.BlockSpec((None, page, KV_LORA), page_map(i)) for i in range(pp)]
    in_specs += [pl.BlockSpec((None, page, QK_ROPE), page_map(i)) for i in range(pp)]
    grid_spec = pltpu.PrefetchScalarGridSpec(
        num_scalar_prefetch=1,
        grid=(bsz, n_pages // pp),
        in_specs=in_specs,
        out_specs=pl.BlockSpec((None, rows, KV_LORA), lambda b, j, pt: (b, 0, 0)),
        scratch_shapes=[pltpu.VMEM((rows, 1), F32), pltpu.VMEM((rows, 1), F32),
                        pltpu.VMEM((rows, KV_LORA), F32)])
    return pl.pallas_call(
        functools.partial(_paged_kernel, pp, t_new),
        grid_spec=grid_spec,
        out_shape=jax.ShapeDtypeStruct((bsz, rows, KV_LORA), BF16),
        compiler_params=_params("parallel", "arbitrary"),
    )(page_table.reshape(-1), q, c_new, pe_new, *([cache_c] * pp), *([cache_pe] * pp))


def _ssd_kernel(lb, xs_ref, b_ref, c_ref, dt_ref, wxs_ref, wb_ref, wc_ref, bxs_ref, bb_ref, bc_ref,
                dtb_ref, alog_ref, dsk_ref, txs_ref, tb_ref, tc_ref, h0_ref,
                y_ref, hout_ref, txs_scr, tb_scr, tc_scr, h_scr):
    L = SSM_CHUNK
    g = pl.program_id(1)
    ci = pl.program_id(2)

    @pl.when(ci == 0)
    def _():
        txs_scr[...] = txs_ref[...]
        tb_scr[...] = tb_ref[...]
        tc_scr[...] = tc_ref[...]
        h_scr[...] = h0_ref[...].reshape(h_scr.shape)

    def pad_rows(x):
        if lb == L:
            return x
        return jnp.concatenate([x, jnp.zeros((L - lb, x.shape[1]), x.dtype)], axis=0)

    def conv_silu(x_ref, tail_scr, w_ref, bias_ref):
        x = pad_rows(x_ref[...].astype(F32))
        tail = tail_scr[...]
        w = w_ref[...]
        row = lax.broadcasted_iota(jnp.int32, (SUBLANES, x.shape[1]), 0)
        out = bias_ref[...] + w[CONV_WIDTH - 1:CONV_WIDTH] * x
        for k in range(1, CONV_WIDTH):
            sh = pltpu.roll(x, k, 0)
            head = jnp.where(row < k, pltpu.roll(tail, k, 0), sh[:SUBLANES])
            sh = jnp.concatenate([head, sh[SUBLANES:]], axis=0)
            out = out + w[CONV_WIDTH - 1 - k:CONV_WIDTH - k] * sh
        tail_scr[...] = x[lb - SUBLANES:lb]
        return _silu(out)

    xs = conv_silu(xs_ref, txs_scr, wxs_ref, bxs_ref)
    bm = conv_silu(b_ref, tb_scr, wb_ref, bb_ref)
    cm = conv_silu(c_ref, tc_scr, wc_ref, bc_ref)

    shift = (LANES - g * HEADS_PER_GROUP) % LANES
    dt_raw = pltpu.roll(pad_rows(dt_ref[...]) + dtb_ref[...], shift, 1)
    a = -jnp.exp(pltpu.roll(alog_ref[...], shift, 1))
    dt = jax.nn.softplus(dt_raw)
    if lb < L:
        dt = jnp.where(lax.broadcasted_iota(jnp.int32, dt.shape, 0) < lb, dt, 0.0)
    la = dt * a

    r_i = lax.broadcasted_iota(jnp.int32, (L, L), 0)
    c_i = lax.broadcasted_iota(jnp.int32, (L, L), 1)
    tril = r_i >= c_i
    cum = _split3_dot(tril.astype(BF16), la)
    cum_t = cum.T
    ecum = jnp.exp(cum)
    to_end = jnp.exp(cum[L - 1:L] - cum)

    e_r = lax.broadcasted_iota(jnp.int32, (LANES, GROUP_CH), 0)
    e_c = lax.broadcasted_iota(jnp.int32, (LANES, GROUP_CH), 1) // SSM_HEAD_DIM
    expand = (e_r == e_c).astype(BF16)
    dt_e = _split2_dot(dt, expand)
    ecum_e = _split2_dot(ecum, expand)
    to_end_e = _split2_dot(to_end, expand)

    xdt = xs * dt_e
    bmb, cmb = bm.astype(BF16), cm.astype(BF16)
    cb = lax.dot_general(cmb, bmb, _NT, preferred_element_type=F32)
    half = lax.broadcasted_iota(jnp.int32, (L, LANES), 1) // SSM_HEAD_DIM
    pairs = []
    for pr in range(HEADS_PER_GROUP // 2):
        xp = xdt[:, pr * LANES:(pr + 1) * LANES]
        acc = jnp.zeros((L, LANES), F32)
        for hh in range(2):
            h = 2 * pr + hh
            seg = cum[:, h:h + 1] - cum_t[h:h + 1, :]
            w = (cb * jnp.exp(jnp.where(tril, seg, -jnp.inf))).astype(BF16)
            acc += _dot(w, jnp.where(half == hh, xp, 0.0).astype(BF16))
        pairs.append(acc)
    y_diag = jnp.concatenate(pairs, axis=1)

    h_prev = h_scr[...]
    y_off = lax.dot_general(cmb, h_prev.astype(BF16), _NT, preferred_element_type=F32) * ecum_e
    y = y_diag + y_off + dsk_ref[...] * xs
    y_ref[...] = y[:lb].astype(y_ref.dtype)

    xw_t = (xdt * to_end_e).T.astype(BF16)
    upd = _dot(xw_t, bmb)
    dec_rows = jnp.exp(jnp.broadcast_to(cum_t[:, L - 1:L], (LANES, SSM_STATE)))
    dec = _split2_dot_left(expand_t(), dec_rows)
    h_new = h_prev * dec + upd
    h_scr[...] = h_new

    @pl.when(ci == pl.num_programs(2) - 1)
    def _():
        hout_ref[...] = h_new.reshape(hout_ref.shape)


def expand_t():
    r = lax.broadcasted_iota(jnp.int32, (GROUP_CH, LANES), 0) // SSM_HEAD_DIM
    c = lax.broadcasted_iota(jnp.int32, (GROUP_CH, LANES), 1)
    return (r == c).astype(BF16)


def _split2_dot_left(m, v):
    hi = v.astype(BF16)
    mid = (v - hi.astype(F32)).astype(BF16)
    return _dot(m, hi) + _dot(m, mid)


def _ssd(xbc, small, conv_w, conv_b, dt_bias, a_log, d_skip, conv_tail, h0, bsz, t):
    lb = min(t, SSM_CHUNK)
    nc = t // lb
    gs, hpg = SSM_GROUPS, HEADS_PER_GROUP
    b_blk0 = SSM_INNER // SSM_STATE
    c_blk0 = b_blk0 + gs
    pad128 = lambda v: jnp.pad(v, (0, LANES - v.shape[0])).reshape(1, LANES)
    dsk = jnp.repeat(d_skip, SSM_HEAD_DIM).reshape(1, SSM_INNER)
    cbias = conv_b.reshape(1, CONV_DIM)

    seq3 = lambda w, col: pl.BlockSpec((None, lb, w), lambda b, g, c: (b, c, col(g)))
    par2 = lambda r, w, col: pl.BlockSpec((r, w), lambda b, g, c: (0, col(g)))
    tail3 = lambda w, col: pl.BlockSpec((None, SUBLANES, w), lambda b, g, c: (b, 0, col(g)))
    xs_col = lambda g: g
    b_col = lambda g: b_blk0 + g
    c_col = lambda g: c_blk0 + g
    zero = lambda g: 0
    state_spec = pl.BlockSpec((None, hpg, SSM_HEAD_DIM, SSM_STATE), lambda b, g, c: (b, g, 0, 0))
    y, h_out = pl.pallas_call(
        functools.partial(_ssd_kernel, lb),
        grid=(bsz, gs, nc),
        in_specs=[seq3(GROUP_CH, xs_col), seq3(SSM_STATE, b_col), seq3(SSM_STATE, c_col),
                  seq3(LANES, lambda g: DT_CHUNK),
                  par2(CONV_WIDTH, GROUP_CH, xs_col), par2(CONV_WIDTH, SSM_STATE, b_col),
                  par2(CONV_WIDTH, SSM_STATE, c_col),
                  par2(1, GROUP_CH, xs_col), par2(1, SSM_STATE, b_col), par2(1, SSM_STATE, c_col),
                  par2(1, LANES, zero), par2(1, LANES, zero), par2(1, GROUP_CH, xs_col),
                  tail3(GROUP_CH, xs_col), tail3(SSM_STATE, b_col), tail3(SSM_STATE, c_col),
                  state_spec],
        out_specs=[seq3(GROUP_CH, xs_col), state_spec],
        out_shape=[jax.ShapeDtypeStruct((bsz, t, SSM_INNER), BF16),
                   jax.ShapeDtypeStruct(h0.shape, F32)],
        scratch_shapes=[pltpu.VMEM((SUBLANES, GROUP_CH), F32), pltpu.VMEM((SUBLANES, SSM_STATE), F32),
                        pltpu.VMEM((SUBLANES, SSM_STATE), F32), pltpu.VMEM((GROUP_CH, SSM_STATE), F32)],
        compiler_params=_params("parallel", "parallel", "arbitrary"),
    )(xbc, xbc, xbc, small, conv_w, conv_w, conv_w, cbias, cbias, cbias,
      pad128(dt_bias), pad128(a_log), dsk, conv_tail, conv_tail, conv_tail, h0)
    return y, h_out


def _mix_kernel(x_ref, attn_ref, y_ref, z_ref, gates_ref, gate_ref, gssm_ref, gpost_ref,
                woa_ref, wos_ref, wout_ref, o_ref):
    o_attn = _dot(attn_ref[...], woa_ref[...])
    yz = y_ref[...].astype(F32) * _silu(z_ref[...].astype(F32))
    o_ssm = _dot(_rms(yz, gssm_ref[...]).astype(BF16), wos_ref[...])
    gates = gates_ref[...].astype(F32)
    mixed = (jax.nn.sigmoid(gates[:, :D_MODEL]) * o_attn
             + jax.nn.sigmoid(gates[:, D_MODEL:]) * o_ssm)
    m2 = _dot(mixed.astype(BF16), wout_ref[...])
    o_ref[...] = x_ref[...] + gate_ref[...] * _rms(m2, gpost_ref[...])


def _mix(x, attn, y, zg, gate, g_ssm, g_post, woa, wos, wout, rows_per_seq):
    n, d = x.shape
    tm = _tile(rows_per_seq if gate.ndim == 3 else n, 256)
    rows = lambda w, col: pl.BlockSpec((tm, w), lambda i: (i, col))
    const = lambda r, c: pl.BlockSpec((r, c), lambda i: (0, 0))
    return pl.pallas_call(
        _mix_kernel,
        grid=(n // tm,),
        in_specs=[rows(d, 0), rows(attn.shape[1], 0), rows(SSM_INNER, 0), rows(SSM_INNER, 0),
                  rows(2 * d, 1), _mod_spec(gate, tm, rows_per_seq, 1),
                  const(1, SSM_INNER), const(1, d),
                  const(*woa.shape), const(*wos.shape), const(*wout.shape)],
        out_specs=rows(d, 0),
        out_shape=jax.ShapeDtypeStruct((n, d), F32),
        compiler_params=_params("parallel"),
    )(x, attn, y, zg, zg, gate, g_ssm.reshape(1, -1), g_post.reshape(1, -1), woa, wos, wout)


def _prep_weights(w_in, w_uq, w_ukv, w_o_attn):
    o = IN_OFFSETS
    seg = lambda i: w_in[:, o[i]:o[i + 1]]
    zeros = lambda c: jnp.zeros((D_MODEL, c), w_in.dtype)
    w_small = jnp.concatenate([seg(0), seg(1), zeros(QK_NOPE), seg(2), zeros(LANES - QK_NOPE - QK_ROPE),
                               seg(5), zeros(LANES - SSM_HEADS)], axis=1)
    w_xbc = seg(4)
    w_zg = jnp.concatenate([seg(3), seg(6), seg(7)], axis=1)
    hd = QK_NOPE + QK_ROPE
    wuq = jnp.pad(w_uq.reshape(Q_LORA, MLA_HEADS, hd), ((0, 0), (0, 0), (0, HEAD_PAD - hd)))
    wuq = wuq.reshape(Q_LORA, MLA_HEADS * HEAD_PAD)
    wkv = w_ukv.reshape(KV_LORA, MLA_HEADS, QK_NOPE + V_HEAD)
    w_uk, w_uv = wkv[..., :QK_NOPE], wkv[..., QK_NOPE:]
    pad_h = lambda w, width: jnp.pad(w, ((0, 0), (0, 0), (0, HEAD_PAD - width)))
    wk = pad_h(w_uk, QK_NOPE).reshape(KV_LORA, MLA_HEADS * HEAD_PAD)
    wv = pad_h(w_uv, V_HEAD).reshape(KV_LORA, MLA_HEADS * HEAD_PAD)
    w_abs = jnp.zeros((MLA_HEADS, HEAD_PAD, KV_LORA + LANES), w_ukv.dtype)
    w_abs = w_abs.at[:, :QK_NOPE, :KV_LORA].set(jnp.transpose(w_uk, (1, 2, 0)))
    w_abs = w_abs.at[:, QK_NOPE:hd, KV_LORA:KV_LORA + QK_ROPE].set(jnp.eye(QK_ROPE, dtype=w_ukv.dtype))
    w_uv_h = pad_h(jnp.transpose(w_uv, (1, 0, 2)), V_HEAD)
    woa = jnp.pad(w_o_attn.reshape(MLA_HEADS, V_HEAD, D_MODEL), ((0, 0), (0, HEAD_PAD - V_HEAD), (0, 0)))
    woa = woa.reshape(MLA_HEADS * HEAD_PAD, D_MODEL)
    cast = lambda w: w.astype(BF16)
    return tuple(map(cast, (w_small, w_xbc, w_zg, wuq, wk, wv, w_abs, w_uv_h, woa)))


def _rope_tables(pos):
    half = QK_ROPE // 2
    inv = ROPE_THETA ** (-jnp.arange(half, dtype=F32) / half)
    ang = pos.astype(F32)[:, None] * inv[None, :]
    cos, sin = jnp.cos(ang), jnp.sin(ang)
    n = pos.shape[0]
    ones, zeros = jnp.ones((n, QK_NOPE), F32), jnp.zeros((n, QK_NOPE), F32)
    tail = jnp.zeros((n, LANES - QK_NOPE - QK_ROPE), F32)
    return (jnp.concatenate([ones, cos, cos, tail], axis=1),
            jnp.concatenate([zeros, -sin, sin, tail], axis=1))


def _layer(x, mods, pos, per_row, conv_prev, ssm_prev, paged, lw):
    (g_pre, g_post, wg, wu, wd, w_small, w_xbc, w_zg, g_q, wuq, g_kv, wk, wv, w_abs, w_uv_h, woa,
     conv_w, conv_b, dt_bias, a_log, d_skip, g_ssm, wos, wout) = lw
    bsz, t, d = x.shape
    n = bsz * t
    x2 = x.reshape(n, d)
    if per_row:
        mod = lambda s, k: jnp.repeat(mods[:, s, k], t, axis=0)
    else:
        mod = lambda s, k: mods[:, s, k][:, None, :]

    x2 = _ffn(x2, mod(0, 0), mod(0, 1), mod(0, 2), g_pre[0], g_post[0], wg[0], wu[0], wd[0], t)

    sh, sc = mod(1, 0), mod(1, 1)
    small = _proj(x2, sh, sc, g_pre[1], w_small, F32, t)
    xbc = _proj(x2, sh, sc, g_pre[1], w_xbc, BF16, t)
    zg = _proj(x2, sh, sc, g_pre[1], w_zg, BF16, t)

    cos, sin = _rope_tables(pos)
    if per_row:
        tm = _tile(n, 512)
        cos, sin = jnp.tile(cos, (tm // t, 1)), jnp.tile(sin, (tm // t, 1))
    mla = _mla(small, cos, sin, g_q, g_kv, wuq, wk, wv, t, paged is None)
    if paged is None:
        q, c_kv, k_pe, k, v = mla
        attn = _prompt_attention(q, k, v, bsz, t)
    else:
        q, c_kv, k_pe = mla
        cache_c, cache_pe, page_table = paged
        q_full = _head_mm(q, w_abs).reshape(bsz, t * MLA_HEADS, KV_LORA + LANES)
        o_lat = _paged_attention(q_full, c_kv.reshape(bsz, t, KV_LORA), k_pe.reshape(bsz, t, QK_ROPE),
                                 cache_c, cache_pe, page_table)
        attn = _head_mm(o_lat.reshape(n, MLA_HEADS * KV_LORA), w_uv_h)

    xbc3 = xbc.reshape(bsz, t, CONV_DIM)
    tail = jnp.pad(conv_prev.astype(F32), ((0, 0), (SUBLANES - (CONV_WIDTH - 1), 0), (0, 0)))
    y, new_ssm = _ssd(xbc3, small.reshape(bsz, t, SMALL_W), conv_w, conv_b, dt_bias, a_log, d_skip,
                      tail, ssm_prev, bsz, t)
    u_tail = jnp.concatenate([conv_prev.astype(F32), xbc3[:, -(CONV_WIDTH - 1):].astype(F32)], axis=1)
    new_conv = u_tail[:, -(CONV_WIDTH - 1):]

    x2 = _mix(x2, attn, y.reshape(n, SSM_INNER), zg, mod(1, 2), g_ssm, g_post[1], woa, wos, wout, t)
    x2 = _ffn(x2, mod(2, 0), mod(2, 1), mod(2, 2), g_pre[2], g_post[2], wg[1], wu[1], wd[1], t)
    return (x2.reshape(bsz, t, d), c_kv.reshape(bsz, t, KV_LORA), k_pe.reshape(bsz, t, QK_ROPE),
            new_conv, new_ssm)


def kernel(x_prompt, x_sample, cache_kv_latent, cache_k_rope, state_conv, state_ssm, page_table,
           c_prompt, c_sample, w_ada, b_ada, g_pre, g_post, w_ffn_gate, w_ffn_up, w_ffn_down, w_in,
           g_q_lat, w_uq, g_kv_lat, w_ukv, w_o_attn, conv_w, conv_b, dt_bias, a_log, d_skip,
           g_ssm_norm, w_o_ssm, w_out):
    bp, t_prompt = x_prompt.shape[:2]
    bs, t_sample = x_sample.shape[:2]
    depth = w_in.shape[0]
    past_len = page_table.shape[1] * cache_kv_latent.shape[2]
    pos_p = jnp.arange(t_prompt, dtype=jnp.int32)
    pos_s = past_len + jnp.arange(t_sample, dtype=jnp.int32)
    yp, ys = x_prompt, x_sample
    outs = [[] for _ in range(8)]
    pad_c = (-(bp + bs)) % SUBLANES
    c_all = jnp.concatenate([c_prompt, c_sample, jnp.zeros((pad_c, D_MODEL), c_prompt.dtype)], axis=0)
    for l in range(depth):
        mods = _ada(c_all, w_ada[l], b_ada[l]).reshape(-1, N_SUB, 3, D_MODEL)
        prepped = _prep_weights(w_in[l], w_uq[l], w_ukv[l], w_o_attn[l])
        w_small, w_xbc, w_zg, wuq, wk, wv, w_abs, w_uv_h, woa = prepped
        cast = lambda w: w.astype(BF16)
        lw = (g_pre[l], g_post[l], cast(w_ffn_gate[l]), cast(w_ffn_up[l]), cast(w_ffn_down[l]),
              w_small, w_xbc, w_zg, g_q_lat[l], wuq, g_kv_lat[l], wk, wv, w_abs, w_uv_h, woa,
              conv_w[l], conv_b[l], dt_bias[l], a_log[l], d_skip[l], g_ssm_norm[l],
              cast(w_o_ssm[l]), cast(w_out[l]))
        conv0 = jnp.zeros((bp, CONV_WIDTH - 1, CONV_DIM), x_prompt.dtype)
        ssm0 = jnp.zeros((bp, SSM_HEADS, SSM_HEAD_DIM, SSM_STATE), state_ssm.dtype)
        res_p = _layer(yp, mods[:bp], pos_p, False, conv0, ssm0, None, lw)
        paged = (cache_kv_latent[l], cache_k_rope[l], page_table)
        res_s = _layer(ys, mods[bp:bp + bs], pos_s, True, state_conv[l], state_ssm[l], paged, lw)
        yp, ys = res_p[0], res_s[0]
        for i in range(4):
            outs[i].append(res_p[1 + i])
            outs[4 + i].append(res_s[1 + i])
    return (yp, ys) + tuple(jnp.stack(o) for o in outs)
```

```python
import functools

import jax
import jax.numpy as jnp
import numpy as np
from jax import lax
from jax.experimental import pallas as pl
from jax.experimental.pallas import tpu as pltpu

D_MODEL = 1024
MLA_HEADS = 16
QK_NOPE = 64
QK_ROPE = 32
V_HEAD = 64
Q_LORA = 512
KV_LORA = 256
ROPE_THETA = 10000.0
ATTN_SCALE = (QK_NOPE + QK_ROPE) ** -0.5
Q_LOG2_SCALE = ATTN_SCALE * float(np.log2(np.e))
SSM_INNER = 2 * D_MODEL
SSM_HEAD_DIM = 64
SSM_HEADS = SSM_INNER // SSM_HEAD_DIM
SSM_GROUPS = 4
SSM_STATE = 128
CONV_WIDTH = 4
SSM_CHUNK = 128
CONV_DIM = SSM_INNER + 2 * SSM_GROUPS * SSM_STATE
D_FF = 2816
FFN_RES = 0.5
N_SUB = 3
EPS = 1e-6
IN_SIZES = (Q_LORA, KV_LORA, QK_ROPE, SSM_INNER, CONV_DIM, SSM_HEADS, D_MODEL, D_MODEL)
IN_OFFSETS = tuple(int(v) for v in np.cumsum((0,) + IN_SIZES))

LANES = 128
SUBLANES = 8
VMEM_LIMIT_BYTES = 56 * 1024 * 1024

HEAD_PAD = LANES
HEADS_PER_GROUP = SSM_HEADS // SSM_GROUPS
GROUP_CH = HEADS_PER_GROUP * SSM_HEAD_DIM
SMALL_W = Q_LORA + KV_LORA + 2 * LANES
KROPE_CHUNK = (Q_LORA + KV_LORA) // LANES
DT_CHUNK = KROPE_CHUNK + 1

BF16 = jnp.bfloat16
F32 = jnp.float32
_NT = (((1,), (1,)), ((), ()))


def _tile(n, target):
    t = min(n, target)
    while n % t:
        t -= 1
    return t


def _params(*sem):
    return pltpu.CompilerParams(dimension_semantics=sem, vmem_limit_bytes=VMEM_LIMIT_BYTES)


def _rms(x, g):
    return x * lax.rsqrt(jnp.mean(x * x, axis=-1, keepdims=True) + EPS) * g


def _silu(x):
    return x * jax.nn.sigmoid(x)


def _dot(a, b):
    return jnp.dot(a, b, preferred_element_type=F32)


def _split2_dot(v, m):
    hi = v.astype(BF16)
    mid = (v - hi.astype(F32)).astype(BF16)
    return _dot(hi, m) + _dot(mid, m)


def _split3_dot(m, v):
    hi = v.astype(BF16)
    r1 = v - hi.astype(F32)
    mid = r1.astype(BF16)
    lo = (r1 - mid.astype(F32)).astype(BF16)
    return _dot(m, hi) + _dot(m, mid) + _dot(m, lo)


def _mod_spec(mod, tm, rows_per_seq, ngrid):
    d = mod.shape[-1]
    if mod.ndim == 3:
        tiles = rows_per_seq // tm
        if ngrid == 1:
            return pl.BlockSpec((None, 1, d), lambda i: (i // tiles, 0, 0))
        return pl.BlockSpec((None, 1, d), lambda i, j: (i // tiles, 0, 0))
    if ngrid == 1:
        return pl.BlockSpec((tm, d), lambda i: (i, 0))
    return pl.BlockSpec((tm, d), lambda i, j: (i, 0))


def _row_spec(width, ngrid):
    if ngrid == 1:
        return pl.BlockSpec((1, width), lambda i: (0, 0))
    return pl.BlockSpec((1, width), lambda i, j: (0, 0))


def _ada_kernel(c_ref, w_ref, b_ref, o_ref):
    h = _silu(c_ref[...]).astype(BF16)
    o_ref[...] = _dot(h, w_ref[...].astype(BF16)) + b_ref[...]


def _ada(c, w, b):
    m, d = c.shape
    n = w.shape[1]
    tn = _tile(n, 1024)
    return pl.pallas_call(
        _ada_kernel,
        grid=(n // tn,),
        in_specs=[pl.BlockSpec((m, d), lambda j: (0, 0)),
                  pl.BlockSpec((d, tn), lambda j: (0, j)),
                  pl.BlockSpec((1, tn), lambda j: (0, j))],
        out_specs=pl.BlockSpec((m, tn), lambda j: (0, j)),
        out_shape=jax.ShapeDtypeStruct((m, n), F32),
        compiler_params=_params("arbitrary"),
    )(c, w, b.reshape(1, n))


def _ffn_kernel(x_ref, shift_ref, scale_ref, gate_ref, gpre_ref, gpost_ref, wg_ref, wu_ref, wd_ref,
                o_ref, h_scr, acc_scr):
    k = pl.program_id(1)

    @pl.when(k == 0)
    def _():
        h = _rms(x_ref[...], gpre_ref[...]) * (1.0 + scale_ref[...]) + shift_ref[...]
        h_scr[...] = h.astype(BF16)
        acc_scr[...] = jnp.zeros_like(acc_scr)

    h = h_scr[...]
    a = _dot(h, wg_ref[...])
    u = _dot(h, wu_ref[...])
    acc_scr[...] += _dot((_silu(a) * u).astype(BF16), wd_ref[...])

    @pl.when(k == pl.num_programs(1) - 1)
    def _():
        f = _rms(acc_scr[...], gpost_ref[...])
        o_ref[...] = x_ref[...] + FFN_RES * gate_ref[...] * f


def _ffn(x, shift, scale, gate, g_pre, g_post, wg, wu, wd, rows_per_seq):
    n, d = x.shape
    dff = wg.shape[1]
    tm = _tile(rows_per_seq if shift.ndim == 3 else n, 1024)
    tf = _tile(dff, 256)
    mspec = lambda m: _mod_spec(m, tm, rows_per_seq, 2)
    return pl.pallas_call(
        _ffn_kernel,
        grid=(n // tm, dff // tf),
        in_specs=[pl.BlockSpec((tm, d), lambda i, k: (i, 0)),
                  mspec(shift), mspec(scale), mspec(gate),
                  _row_spec(d, 2), _row_spec(d, 2),
                  pl.BlockSpec((d, tf), lambda i, k: (0, k)),
                  pl.BlockSpec((d, tf), lambda i, k: (0, k)),
                  pl.BlockSpec((tf, d), lambda i, k: (k, 0))],
        out_specs=pl.BlockSpec((tm, d), lambda i, k: (i, 0)),
        out_shape=jax.ShapeDtypeStruct((n, d), F32),
        scratch_shapes=[pltpu.VMEM((tm, d), BF16), pltpu.VMEM((tm, d), F32)],
        compiler_params=_params("parallel", "arbitrary"),
    )(x, shift, scale, gate, g_pre.reshape(1, d), g_post.reshape(1, d), wg, wu, wd)


def _proj_kernel(x_ref, shift_ref, scale_ref, gpre_ref, w_ref, o_ref, h_scr):
    @pl.when(pl.program_id(1) == 0)
    def _():
        h = _rms(x_ref[...], gpre_ref[...]) * (1.0 + scale_ref[...]) + shift_ref[...]
        h_scr[...] = h.astype(BF16)

    o_ref[...] = _dot(h_scr[...], w_ref[...]).astype(o_ref.dtype)


def _proj(x, shift, scale, g_pre, w, out_dtype, rows_per_seq):
    n, d = x.shape
    nout = w.shape[1]
    tm = _tile(rows_per_seq if shift.ndim == 3 else n, 1024)
    tn = _tile(nout, 1024)
    mspec = lambda m: _mod_spec(m, tm, rows_per_seq, 2)
    return pl.pallas_call(
        _proj_kernel,
        grid=(n // tm, nout // tn),
        in_specs=[pl.BlockSpec((tm, d), lambda i, j: (i, 0)),
                  mspec(shift), mspec(scale), _row_spec(d, 2),
                  pl.BlockSpec((d, tn), lambda i, j: (0, j))],
        out_specs=pl.BlockSpec((tm, tn), lambda i, j: (i, j)),
        out_shape=jax.ShapeDtypeStruct((n, nout), out_dtype),
        scratch_shapes=[pltpu.VMEM((tm, d), BF16)],
        compiler_params=_params("parallel", "arbitrary"),
    )(x, shift, scale, g_pre.reshape(1, d), w)


def _rope(x, cos, sin):
    w = x.shape[1]
    lane = lax.broadcasted_iota(jnp.int32, x.shape, 1) % LANES
    first_half = (lane >= QK_NOPE) & (lane < QK_NOPE + QK_ROPE // 2)
    swapped = jnp.where(first_half, pltpu.roll(x, w - QK_ROPE // 2, 1), pltpu.roll(x, QK_ROPE // 2, 1))
    return x * cos + swapped * sin


def _mla_kernel(prompt, small_ref, cos_ref, sin_ref, gq_ref, gkv_ref, wuq_ref, *rest):
    if prompt:
        wk_ref, wv_ref, q_ref, ckv_ref, kpe_ref, k_ref, v_ref = rest
    else:
        q_ref, ckv_ref, kpe_ref = rest
    sm = small_ref[...]
    cos, sin = cos_ref[...], sin_ref[...]
    qn = _rms(sm[:, :Q_LORA], gq_ref[...])
    q = _dot(qn.astype(BF16), wuq_ref[...])
    q = _rope(q, jnp.tile(cos, (1, MLA_HEADS)), jnp.tile(sin, (1, MLA_HEADS)))
    q_ref[...] = (q * Q_LOG2_SCALE).astype(BF16)
    ckv = _rms(sm[:, Q_LORA:Q_LORA + KV_LORA], gkv_ref[...])
    ckv_ref[...] = ckv
    kr = _rope(sm[:, KROPE_CHUNK * LANES:(KROPE_CHUNK + 1) * LANES], cos, sin)
    kpe_ref[...] = kr[:, QK_NOPE:QK_NOPE + QK_ROPE]
    if prompt:
        cb = ckv.astype(BF16)
        k_ref[...] = (_dot(cb, wk_ref[...]) + jnp.tile(kr, (1, MLA_HEADS))).astype(BF16)
        v_ref[...] = _dot(cb, wv_ref[...]).astype(BF16)


def _mla(small, cos, sin, g_q, g_kv, wuq, wk, wv, rows_per_seq, prompt):
    n = small.shape[0]
    hw = MLA_HEADS * HEAD_PAD
    tm = _tile(rows_per_seq if prompt else n, 512)
    tiles = rows_per_seq // tm if prompt else 1
    tab = pl.BlockSpec((tm, LANES), (lambda i: (i % tiles, 0)) if prompt else (lambda i: (0, 0)))
    const = lambda r, c: pl.BlockSpec((r, c), lambda i: (0, 0))
    rows = lambda c: pl.BlockSpec((tm, c), lambda i: (i, 0))
    in_specs = [rows(SMALL_W), tab, tab, const(1, Q_LORA), const(1, KV_LORA), const(Q_LORA, hw)]
    args = [small, cos, sin, g_q.reshape(1, -1), g_kv.reshape(1, -1), wuq]
    out_specs = [rows(hw), rows(KV_LORA), rows(QK_ROPE)]
    out_shape = [jax.ShapeDtypeStruct((n, hw), BF16), jax.ShapeDtypeStruct((n, KV_LORA), F32),
                 jax.ShapeDtypeStruct((n, QK_ROPE), F32)]
    if prompt:
        in_specs += [const(KV_LORA, hw), const(KV_LORA, hw)]
        args += [wk, wv]
        out_specs += [rows(hw), rows(hw)]
        out_shape += [jax.ShapeDtypeStruct((n, hw), BF16)] * 2
    return pl.pallas_call(
        functools.partial(_mla_kernel, prompt),
        grid=(n // tm,),
        in_specs=in_specs, out_specs=out_specs, out_shape=out_shape,
        compiler_params=_params("parallel"),
    )(*args)


def _attn_kernel(tq, q_ref, k_ref, v_ref, o_ref):
    j = pl.program_id(2)
    causal = (lax.broadcasted_iota(jnp.int32, (tq, tq), 1) <= lax.broadcasted_iota(jnp.int32, (tq, tq), 0))

    def one_tile(parity):
        q = q_ref[parity * tq:(parity + 1) * tq, :]

        def scores(kb):
            k = k_ref[pl.ds(pl.multiple_of(kb * tq, tq), tq), :]
            return lax.dot_general(q, k, _NT, preferred_element_type=F32)

        def fold(state, s, kb):
            m, l, acc = state
            v = v_ref[pl.ds(pl.multiple_of(kb * tq, tq), tq), :]
            m_new = jnp.maximum(m, jnp.max(s, axis=-1, keepdims=True))
            alpha = jnp.exp2(m - m_new)
            p = jnp.exp2(s - m_new)
            l = alpha * l + jnp.sum(p, axis=-1, keepdims=True)
            return m_new, l, alpha * acc + _dot(p.astype(BF16), v)

        def pair(i, state):
            s0, s1 = scores(2 * i), scores(2 * i + 1)
            return fold(fold(state, s0, 2 * i), s1, 2 * i + 1)

        init = (jnp.full((tq, 1), -jnp.inf, F32), jnp.zeros((tq, 1), F32), jnp.zeros((tq, HEAD_PAD), F32))
        state = lax.fori_loop(0, j, pair, init)
        diag = 2 * j + parity
        if parity:
            s_full, s_diag = scores(2 * j), scores(diag)
            state = fold(state, s_full, 2 * j)
        else:
            s_diag = scores(diag)
        _, l, acc = fold(state, jnp.where(causal, s_diag, -jnp.inf), diag)
        o_ref[parity * tq:(parity + 1) * tq, :] = (acc / l).astype(BF16)

    for parity in range(q_ref.shape[0] // tq):
        one_tile(parity)


def _prompt_attention(q, k, v, bsz, t):
    assert t % 2 == 0, t
    tq = _tile(t // 2, 512)
    nq = t // (2 * tq)
    return pl.pallas_call(
        functools.partial(_attn_kernel, tq),
        grid=(bsz, MLA_HEADS, nq),
        in_specs=[pl.BlockSpec((2 * tq, HEAD_PAD), lambda b, h, i: (b * nq + i, h)),
                  pl.BlockSpec((t, HEAD_PAD), lambda b, h, i: (b, h)),
                  pl.BlockSpec((t, HEAD_PAD), lambda b, h, i: (b, h))],
        out_specs=pl.BlockSpec((2 * tq, HEAD_PAD), lambda b, h, i: (b * nq + i, h)),
        out_shape=jax.ShapeDtypeStruct(q.shape, BF16),
        compiler_params=_params("parallel", "parallel", "arbitrary"),
    )(q, k, v)


def _head_mm_kernel(x_ref, w_ref, o_ref):
    o_ref[...] = _dot(x_ref[...], w_ref[...]).astype(o_ref.dtype)


def _head_mm(x, w):
    n = x.shape[0]
    nh, kin, kout = w.shape
    return pl.pallas_call(
        _head_mm_kernel,
        grid=(nh,),
        in_specs=[pl.BlockSpec((n, kin), lambda h: (0, h)),
                  pl.BlockSpec((None, kin, kout), lambda h: (h, 0, 0))],
        out_specs=pl.BlockSpec((n, kout), lambda h: (0, h)),
        out_shape=jax.ShapeDtypeStruct((n, nh * kout), BF16),
        compiler_params=_params("parallel"),
    )(x, w)


def _paged_kernel(pp, sub, per_seq, n_chunks, t_new, pt_ref, q_ref, cnew_ref, penew_ref, lat_hbm, pe_hbm,
                  o_ref, lat_buf, pe_buf, sems):
    b = pl.program_id(0)
    rows = q_ref.shape[0]

    def copies(chunk, slot, i):
        page = pt_ref[chunk * pp + i]
        return (pltpu.make_async_copy(lat_hbm.at[page], lat_buf.at[slot, i], sems.at[0, slot]),
                pltpu.make_async_copy(pe_hbm.at[page], pe_buf.at[slot, i], sems.at[1, slot]))

    def fetch(chunk, slot):
        for i in range(pp):
            for cp in copies(chunk, slot, i):
                cp.start()

    def wait(chunk, slot):
        for i in range(pp):
            for cp in copies(chunk, slot, i):
                cp.wait()

    @pl.when(b == 0)
    def _():
        fetch(0, 0)

    q = q_ref[...]
    q_abs = q[:, :KV_LORA]
    q_pe = q[:, KV_LORA:KV_LORA + QK_ROPE]

    def fold(state, s, c):
        m_old, l_old, acc = state
        m_new = jnp.maximum(m_old, jnp.max(s, axis=-1, keepdims=True))
        alpha = jnp.exp2(m_old - m_new)
        p = jnp.exp2(s - m_new)
        l_new = alpha * l_old + jnp.sum(p, axis=-1, keepdims=True)
        return m_new, l_new, alpha * acc + _dot(p.astype(BF16), c)

    def scores(slot, g0):
        c = jnp.concatenate([lat_buf[slot, i].astype(BF16) for i in range(g0, g0 + sub)], axis=0)
        pe_t = jnp.concatenate([pe_buf[slot, i].astype(BF16) for i in range(g0, g0 + sub)], axis=1)
        return lax.dot_general(q_abs, c, _NT, preferred_element_type=F32) + _dot(q_pe, pe_t), c

    def consume(slot, state):
        pending = scores(slot, 0)
        for g0 in range(sub, pp, sub):
            upcoming = scores(slot, g0)
            state = fold(state, *pending)
            pending = upcoming
        return fold(state, *pending)

    def pair(jj, state):
        for slot in range(2):
            chunk = b * per_seq + 2 * jj + slot

            @pl.when(chunk + 1 < n_chunks)
            def _():
                fetch(chunk + 1, 1 - slot)

            wait(chunk, slot)
            state = consume(slot, state)
        return state

    init = (jnp.full((rows, 1), -jnp.inf, F32), jnp.zeros((rows, 1), F32), jnp.zeros((rows, KV_LORA), F32))
    state = lax.fori_loop(0, per_seq // 2, pair, init)

    pad = LANES - t_new
    c = jnp.concatenate([cnew_ref[...], jnp.zeros((pad, KV_LORA), F32)], axis=0).astype(BF16)
    pe = jnp.concatenate([penew_ref[...], jnp.zeros((pad, QK_ROPE), F32)], axis=0).astype(BF16)
    s_new = (lax.dot_general(q_abs, c, _NT, preferred_element_type=F32)
             + lax.dot_general(q_pe, pe, _NT, preferred_element_type=F32))
    t_row = lax.broadcasted_iota(jnp.int32, (rows, LANES), 0) // MLA_HEADS
    key = lax.broadcasted_iota(jnp.int32, (rows, LANES), 1)
    _, l, acc = fold(state, jnp.where(key <= t_row, s_new, -jnp.inf), c)
    o_ref[...] = (acc / l).astype(BF16)


def _paged_attention(q, c_new, pe_new, cache_c, cache_pe_t, page_table):
    bsz, rows, qw = q.shape
    t_new = c_new.shape[1]
    n_pages = page_table.shape[1]
    page = cache_c.shape[1]
    assert n_pages % 2 == 0, n_pages
    pp = _tile(n_pages // 2, 16)
    sub = _tile(pp, 4)
    per_seq = n_pages // pp
    seq = lambda w0, w1: pl.BlockSpec((None, w0, w1), lambda b, pt: (b, 0, 0))
    grid_spec = pltpu.PrefetchScalarGridSpec(
        num_scalar_prefetch=1,
        grid=(bsz,),
        in_specs=[seq(rows, qw), seq(t_new, KV_LORA), seq(t_new, QK_ROPE),
                  pl.BlockSpec(memory_space=pl.ANY), pl.BlockSpec(memory_space=pl.ANY)],
        out_specs=seq(rows, KV_LORA),
        scratch_shapes=[pltpu.VMEM((2, pp, page, KV_LORA), F32), pltpu.VMEM((2, pp, QK_ROPE, page), F32),
                        pltpu.SemaphoreType.DMA((2, 2))])
    return pl.pallas_call(
        functools.partial(_paged_kernel, pp, sub, per_seq, bsz * per_seq, t_new),
        grid_spec=grid_spec,
        out_shape=jax.ShapeDtypeStruct((bsz, rows, KV_LORA), BF16),
        compiler_params=_params("arbitrary"),
    )(page_table.reshape(-1), q, c_new, pe_new, cache_c, cache_pe_t)


def _ssd_kernel(lb, nb, xs_ref, b_ref, c_ref, dt_ref, wxs_ref, wb_ref, wc_ref, bxs_ref, bb_ref, bc_ref,
                dtb_ref, alog_ref, dsk_ref, txs_ref, tb_ref, tc_ref, h0_ref,
                y_ref, hout_ref, txs_scr, tb_scr, tc_scr, h_scr):
    ci = pl.program_id(2)

    @pl.when(ci == 0)
    def _():
        txs_scr[...] = txs_ref[...]
        tb_scr[...] = tb_ref[...]
        tc_scr[...] = tc_ref[...]
        h_scr[...] = h0_ref[...].reshape(h_scr.shape)

    for s in range(nb):
        _ssd_chunk(lb, s, xs_ref, b_ref, c_ref, dt_ref, wxs_ref, wb_ref, wc_ref, bxs_ref, bb_ref, bc_ref,
                   dtb_ref, alog_ref, dsk_ref, y_ref, txs_scr, tb_scr, tc_scr, h_scr)

    @pl.when(ci == pl.num_programs(2) - 1)
    def _():
        hout_ref[...] = h_scr[...].reshape(hout_ref.shape)


def _ssd_chunk(lb, s, xs_ref, b_ref, c_ref, dt_ref, wxs_ref, wb_ref, wc_ref, bxs_ref, bb_ref, bc_ref,
               dtb_ref, alog_ref, dsk_ref, y_ref, txs_scr, tb_scr, tc_scr, h_scr):
    L = max(lb, 2 * SUBLANES)
    g = pl.program_id(1)

    def pad_rows(x):
        if lb == L:
            return x
        return jnp.concatenate([x, jnp.zeros((L - lb, x.shape[1]), x.dtype)], axis=0)

    def conv_silu(x_ref, tail_scr, w_ref, bias_ref):
        x = pad_rows(x_ref[s].astype(F32))
        tail = tail_scr[s]
        w = w_ref[...]
        row = lax.broadcasted_iota(jnp.int32, (SUBLANES, x.shape[1]), 0)
        out = bias_ref[...] + w[CONV_WIDTH - 1:CONV_WIDTH] * x
        for k in range(1, CONV_WIDTH):
            sh = pltpu.roll(x, k, 0)
            head = jnp.where(row < k, pltpu.roll(tail, k, 0), sh[:SUBLANES])
            sh = jnp.concatenate([head, sh[SUBLANES:]], axis=0)
            out = out + w[CONV_WIDTH - 1 - k:CONV_WIDTH - k] * sh
        tail_scr[s] = x[lb - SUBLANES:lb]
        return _silu(out)

    xs = conv_silu(xs_ref, txs_scr, wxs_ref, bxs_ref)
    bm = conv_silu(b_ref, tb_scr, wb_ref, bb_ref)
    cm = conv_silu(c_ref, tc_scr, wc_ref, bc_ref)

    shift = (LANES - g * HEADS_PER_GROUP) % LANES
    dt_raw = pltpu.roll(pad_rows(dt_ref[s]) + dtb_ref[...], shift, 1)
    a = -jnp.exp(pltpu.roll(alog_ref[...], shift, 1))
    dt = jax.nn.softplus(dt_raw)
    if lb < L:
        dt = jnp.where(lax.broadcasted_iota(jnp.int32, dt.shape, 0) < lb, dt, 0.0)
    la = dt * a

    r_i = lax.broadcasted_iota(jnp.int32, (L, L), 0)
    c_i = lax.broadcasted_iota(jnp.int32, (L, L), 1)
    tril = r_i >= c_i
    cum = _split3_dot(tril.astype(BF16), la)
    cum_t = cum.T
    ecum = jnp.exp(cum)
    to_end = jnp.exp(cum[L - 1:L] - cum)

    e_r = lax.broadcasted_iota(jnp.int32, (LANES, GROUP_CH), 0)
    e_c = lax.broadcasted_iota(jnp.int32, (LANES, GROUP_CH), 1) // SSM_HEAD_DIM
    expand = (e_r == e_c).astype(BF16)
    dt_e = _split2_dot(dt, expand)
    ecum_e = _split2_dot(ecum, expand)
    to_end_e = _split2_dot(to_end, expand)

    xdt = xs * dt_e
    bmb, cmb = bm.astype(BF16), cm.astype(BF16)
    cb = lax.dot_general(cmb, bmb, _NT, preferred_element_type=F32)
    half = lax.broadcasted_iota(jnp.int32, (L, LANES), 1) // SSM_HEAD_DIM
    pairs = []
    for pr in range(HEADS_PER_GROUP // 2):
        xp = xdt[:, pr * LANES:(pr + 1) * LANES]
        acc = jnp.zeros((L, LANES), F32)
        for hh in range(2):
            h = 2 * pr + hh
            seg = cum[:, h:h + 1] - cum_t[h:h + 1, :]
            w = (cb * jnp.exp(jnp.where(tril, seg, -jnp.inf))).astype(BF16)
            acc += _dot(w, jnp.where(half == hh, xp, 0.0).astype(BF16))
        pairs.append(acc)
    y_diag = jnp.concatenate(pairs, axis=1)

    h_prev = h_scr[s]
    y_off = lax.dot_general(cmb, h_prev.astype(BF16), _NT, preferred_element_type=F32) * ecum_e
    y = y_diag + y_off + dsk_ref[...] * xs
    y_ref[s] = y[:lb].astype(y_ref.dtype)

    xw_t = (xdt * to_end_e).T.astype(BF16)
    upd = _dot(xw_t, bmb)
    d_last = jnp.exp(cum_t[:, L - 1:L])
    dec = jnp.concatenate([jnp.broadcast_to(d_last[h:h + 1, :], (SSM_HEAD_DIM, SSM_STATE))
                           for h in range(HEADS_PER_GROUP)], axis=0)
    h_scr[s] = h_prev * dec + upd


def _ssd(xbc, small, conv_w, conv_b, dt_bias, a_log, d_skip, conv_tail, h0, bsz, t):
    lb = min(t, SSM_CHUNK)
    nc = t // lb
    gs, hpg = SSM_GROUPS, HEADS_PER_GROUP
    b_blk0 = SSM_INNER // SSM_STATE
    c_blk0 = b_blk0 + gs
    pad128 = lambda v: jnp.pad(v, (0, LANES - v.shape[0])).reshape(1, LANES)
    dsk = jnp.repeat(d_skip, SSM_HEAD_DIM).reshape(1, SSM_INNER)
    cbias = conv_b.reshape(1, CONV_DIM)

    nb = _tile(bsz, 4)
    seq3 = lambda w, col: pl.BlockSpec((nb, lb, w), lambda b, g, c: (b, c, col(g)))
    par2 = lambda r, w, col: pl.BlockSpec((r, w), lambda b, g, c: (0, col(g)))
    tail3 = lambda w, col: pl.BlockSpec((nb, SUBLANES, w), lambda b, g, c: (b, 0, col(g)))
    xs_col = lambda g: g
    b_col = lambda g: b_blk0 + g
    c_col = lambda g: c_blk0 + g
    zero = lambda g: 0
    state_spec = pl.BlockSpec((nb, hpg, SSM_HEAD_DIM, SSM_STATE), lambda b, g, c: (b, g, 0, 0))
    y, h_out = pl.pallas_call(
        functools.partial(_ssd_kernel, lb, nb),
        grid=(bsz // nb, gs, nc),
        in_specs=[seq3(GROUP_CH, xs_col), seq3(SSM_STATE, b_col), seq3(SSM_STATE, c_col),
                  seq3(LANES, lambda g: DT_CHUNK),
                  par2(CONV_WIDTH, GROUP_CH, xs_col), par2(CONV_WIDTH, SSM_STATE, b_col),
                  par2(CONV_WIDTH, SSM_STATE, c_col),
                  par2(1, GROUP_CH, xs_col), par2(1, SSM_STATE, b_col), par2(1, SSM_STATE, c_col),
                  par2(1, LANES, zero), par2(1, LANES, zero), par2(1, GROUP_CH, xs_col),
                  tail3(GROUP_CH, xs_col), tail3(SSM_STATE, b_col), tail3(SSM_STATE, c_col),
                  state_spec],
        out_specs=[seq3(GROUP_CH, xs_col), state_spec],
        out_shape=[jax.ShapeDtypeStruct((bsz, t, SSM_INNER), BF16),
                   jax.ShapeDtypeStruct(h0.shape, F32)],
        scratch_shapes=[pltpu.VMEM((nb, SUBLANES, GROUP_CH), F32), pltpu.VMEM((nb, SUBLANES, SSM_STATE), F32),
                        pltpu.VMEM((nb, SUBLANES, SSM_STATE), F32),
                        pltpu.VMEM((nb, GROUP_CH, SSM_STATE), F32)],
        compiler_params=_params("parallel", "parallel", "arbitrary"),
    )(xbc, xbc, xbc, small, conv_w, conv_w, conv_w, cbias, cbias, cbias,
      pad128(dt_bias), pad128(a_log), dsk, conv_tail, conv_tail, conv_tail, h0)
    return y, h_out


def _mix_kernel(x_ref, attn_ref, y_ref, z_ref, gates_ref, gate_ref, gssm_ref, gpost_ref,
                woa_ref, wos_ref, wout_ref, o_ref):
    o_attn = _dot(attn_ref[...], woa_ref[...])
    yz = y_ref[...].astype(F32) * _silu(z_ref[...].astype(F32))
    o_ssm = _dot(_rms(yz, gssm_ref[...]).astype(BF16), wos_ref[...])
    gates = gates_ref[...].astype(F32)
    mixed = (jax.nn.sigmoid(gates[:, :D_MODEL]) * o_attn
             + jax.nn.sigmoid(gates[:, D_MODEL:]) * o_ssm)
    m2 = _dot(mixed.astype(BF16), wout_ref[...])
    o_ref[...] = x_ref[...] + gate_ref[...] * _rms(m2, gpost_ref[...])


def _mix(x, attn, y, zg, gate, g_ssm, g_post, woa, wos, wout, rows_per_seq):
    n, d = x.shape
    tm = _tile(rows_per_seq if gate.ndim == 3 else n, 256)
    rows = lambda w, col: pl.BlockSpec((tm, w), lambda i: (i, col))
    const = lambda r, c: pl.BlockSpec((r, c), lambda i: (0, 0))
    return pl.pallas_call(
        _mix_kernel,
        grid=(n // tm,),
        in_specs=[rows(d, 0), rows(attn.shape[1], 0), rows(SSM_INNER, 0), rows(SSM_INNER, 0),
                  rows(2 * d, 1), _mod_spec(gate, tm, rows_per_seq, 1),
                  const(1, SSM_INNER), const(1, d),
                  const(*woa.shape), const(*wos.shape), const(*wout.shape)],
        out_specs=rows(d, 0),
        out_shape=jax.ShapeDtypeStruct((n, d), F32),
        compiler_params=_params("parallel"),
    )(x, attn, y, zg, zg, gate, g_ssm.reshape(1, -1), g_post.reshape(1, -1), woa, wos, wout)


def _prep_weights(w_in, w_uq, w_ukv, w_o_attn):
    o = IN_OFFSETS
    seg = lambda i: w_in[:, o[i]:o[i + 1]]
    zeros = lambda c: jnp.zeros((D_MODEL, c), w_in.dtype)
    w_small = jnp.concatenate([seg(0), seg(1), zeros(QK_NOPE), seg(2), zeros(LANES - QK_NOPE - QK_ROPE),
                               seg(5), zeros(LANES - SSM_HEADS)], axis=1)
    w_xbc = seg(4)
    w_zg = jnp.concatenate([seg(3), seg(6), seg(7)], axis=1)
    hd = QK_NOPE + QK_ROPE
    wuq = jnp.pad(w_uq.reshape(Q_LORA, MLA_HEADS, hd), ((0, 0), (0, 0), (0, HEAD_PAD - hd)))
    wuq = wuq.reshape(Q_LORA, MLA_HEADS * HEAD_PAD)
    wkv = w_ukv.reshape(KV_LORA, MLA_HEADS, QK_NOPE + V_HEAD)
    w_uk, w_uv = wkv[..., :QK_NOPE], wkv[..., QK_NOPE:]
    pad_h = lambda w, width: jnp.pad(w, ((0, 0), (0, 0), (0, HEAD_PAD - width)))
    wk = pad_h(w_uk, QK_NOPE).reshape(KV_LORA, MLA_HEADS * HEAD_PAD)
    wv = pad_h(w_uv, V_HEAD).reshape(KV_LORA, MLA_HEADS * HEAD_PAD)
    w_abs = jnp.zeros((MLA_HEADS, HEAD_PAD, KV_LORA + LANES), w_ukv.dtype)
    w_abs = w_abs.at[:, :QK_NOPE, :KV_LORA].set(jnp.transpose(w_uk, (1, 2, 0)))
    w_abs = w_abs.at[:, QK_NOPE:hd, KV_LORA:KV_LORA + QK_ROPE].set(jnp.eye(QK_ROPE, dtype=w_ukv.dtype))
    w_uv_h = pad_h(jnp.transpose(w_uv, (1, 0, 2)), V_HEAD)
    woa = jnp.pad(w_o_attn.reshape(MLA_HEADS, V_HEAD, D_MODEL), ((0, 0), (0, HEAD_PAD - V_HEAD), (0, 0)))
    woa = woa.reshape(MLA_HEADS * HEAD_PAD, D_MODEL)
    cast = lambda w: w.astype(BF16)
    return tuple(map(cast, (w_small, w_xbc, w_zg, wuq, wk, wv, w_abs, w_uv_h, woa)))


def _rope_tables(pos):
    half = QK_ROPE // 2
    inv = ROPE_THETA ** (-jnp.arange(half, dtype=F32) / half)
    ang = pos.astype(F32)[:, None] * inv[None, :]
    cos, sin = jnp.cos(ang), jnp.sin(ang)
    n = pos.shape[0]
    ones, zeros = jnp.ones((n, QK_NOPE), F32), jnp.zeros((n, QK_NOPE), F32)
    tail = jnp.zeros((n, LANES - QK_NOPE - QK_ROPE), F32)
    return (jnp.concatenate([ones, cos, cos, tail], axis=1),
            jnp.concatenate([zeros, -sin, sin, tail], axis=1))


def _layer(x, mods, pos, per_row, conv_prev, ssm_prev, paged, lw):
    (g_pre, g_post, wg, wu, wd, w_small, w_xbc, w_zg, g_q, wuq, g_kv, wk, wv, w_abs, w_uv_h, woa,
     conv_w, conv_b, dt_bias, a_log, d_skip, g_ssm, wos, wout) = lw
    bsz, t, d = x.shape
    n = bsz * t
    x2 = x.reshape(n, d)
    if per_row:
        mod = lambda s, k: jnp.repeat(mods[:, s, k], t, axis=0)
    else:
        mod = lambda s, k: mods[:, s, k][:, None, :]

    x2 = _ffn(x2, mod(0, 0), mod(0, 1), mod(0, 2), g_pre[0], g_post[0], wg[0], wu[0], wd[0], t)

    sh, sc = mod(1, 0), mod(1, 1)
    small = _proj(x2, sh, sc, g_pre[1], w_small, F32, t)
    xbc = _proj(x2, sh, sc, g_pre[1], w_xbc, BF16, t)
    zg = _proj(x2, sh, sc, g_pre[1], w_zg, BF16, t)

    cos, sin = _rope_tables(pos)
    if per_row:
        tm = _tile(n, 512)
        cos, sin = jnp.tile(cos, (tm // t, 1)), jnp.tile(sin, (tm // t, 1))
    mla = _mla(small, cos, sin, g_q, g_kv, wuq, wk, wv, t, paged is None)
    if paged is None:
        q, c_kv, k_pe, k, v = mla
        attn = _prompt_attention(q, k, v, bsz, t)
    else:
        q, c_kv, k_pe = mla
        cache_c, cache_pe, page_table = paged
        q_full = _head_mm(q, w_abs).reshape(bsz, t * MLA_HEADS, KV_LORA + LANES)
        o_lat = _paged_attention(q_full, c_kv.reshape(bsz, t, KV_LORA), k_pe.reshape(bsz, t, QK_ROPE),
                                 cache_c, cache_pe, page_table)
        attn = _head_mm(o_lat.reshape(n, MLA_HEADS * KV_LORA), w_uv_h)

    xbc3 = xbc.reshape(bsz, t, CONV_DIM)
    tail = jnp.pad(conv_prev.astype(F32), ((0, 0), (SUBLANES - (CONV_WIDTH - 1), 0), (0, 0)))
    y, new_ssm = _ssd(xbc3, small.reshape(bsz, t, SMALL_W), conv_w, conv_b, dt_bias, a_log, d_skip,
                      tail, ssm_prev, bsz, t)
    u_tail = jnp.concatenate([conv_prev.astype(F32), xbc3[:, -(CONV_WIDTH - 1):].astype(F32)], axis=1)
    new_conv = u_tail[:, -(CONV_WIDTH - 1):]

    x2 = _mix(x2, attn, y.reshape(n, SSM_INNER), zg, mod(1, 2), g_ssm, g_post[1], woa, wos, wout, t)
    x2 = _ffn(x2, mod(2, 0), mod(2, 1), mod(2, 2), g_pre[2], g_post[2], wg[1], wu[1], wd[1], t)
    return (x2.reshape(bsz, t, d), c_kv.reshape(bsz, t, KV_LORA), k_pe.reshape(bsz, t, QK_ROPE),
            new_conv, new_ssm)


def kernel(x_prompt, x_sample, cache_kv_latent, cache_k_rope, state_conv, state_ssm, page_table,
           c_prompt, c_sample, w_ada, b_ada, g_pre, g_post, w_ffn_gate, w_ffn_up, w_ffn_down, w_in,
           g_q_lat, w_uq, g_kv_lat, w_ukv, w_o_attn, conv_w, conv_b, dt_bias, a_log, d_skip,
           g_ssm_norm, w_o_ssm, w_out):
    bp, t_prompt = x_prompt.shape[:2]
    bs, t_sample = x_sample.shape[:2]
    depth = w_in.shape[0]
    past_len = page_table.shape[1] * cache_kv_latent.shape[2]
    pos_p = jnp.arange(t_prompt, dtype=jnp.int32)
    pos_s = past_len + jnp.arange(t_sample, dtype=jnp.int32)
    yp, ys = x_prompt, x_sample
    outs = [[] for _ in range(8)]
    pad_c = (-(bp + bs)) % SUBLANES
    c_all = jnp.concatenate([c_prompt, c_sample, jnp.zeros((pad_c, D_MODEL), c_prompt.dtype)], axis=0)
    for l in range(depth):
        mods = _ada(c_all, w_ada[l], b_ada[l]).reshape(-1, N_SUB, 3, D_MODEL)
        prepped = _prep_weights(w_in[l], w_uq[l], w_ukv[l], w_o_attn[l])
        w_small, w_xbc, w_zg, wuq, wk, wv, w_abs, w_uv_h, woa = prepped
        cast = lambda w: w.astype(BF16)
        lw = (g_pre[l], g_post[l], cast(w_ffn_gate[l]), cast(w_ffn_up[l]), cast(w_ffn_down[l]),
              w_small, w_xbc, w_zg, g_q_lat[l], wuq, g_kv_lat[l], wk, wv, w_abs, w_uv_h, woa,
              conv_w[l], conv_b[l], dt_bias[l], a_log[l], d_skip[l], g_ssm_norm[l],
              cast(w_o_ssm[l]), cast(w_out[l]))
        conv0 = jnp.zeros((bp, CONV_WIDTH - 1, CONV_DIM), x_prompt.dtype)
        ssm0 = jnp.zeros((bp, SSM_HEADS, SSM_HEAD_DIM, SSM_STATE), state_ssm.dtype)
        res_p = _layer(yp, mods[:bp], pos_p, False, conv0, ssm0, None, lw)
        paged = (cache_kv_latent[l], jnp.swapaxes(cache_k_rope[l], 1, 2), page_table)
        res_s = _layer(ys, mods[bp:bp + bs], pos_s, True, state_conv[l], state_ssm[l], paged, lw)
        yp, ys = res_p[0], res_s[0]
        for i in range(4):
            outs[i].append(res_p[1 + i])
            outs[4 + i].append(res_s[1 + i])
    return (yp, ys) + tuple(jnp.stack(o) for o in outs)
```

```python
import functools

import jax
import jax.numpy as jnp
import numpy as np
from jax import lax
from jax.experimental import pallas as pl
from jax.experimental.pallas import tpu as pltpu

D_MODEL = 1024
MLA_HEADS = 16
QK_NOPE = 64
QK_ROPE = 32
V_HEAD = 64
Q_LORA = 512
KV_LORA = 256
ROPE_THETA = 10000.0
ATTN_SCALE = (QK_NOPE + QK_ROPE) ** -0.5
Q_LOG2_SCALE = ATTN_SCALE * float(np.log2(np.e))
SSM_INNER = 2 * D_MODEL
SSM_HEAD_DIM = 64
SSM_HEADS = SSM_INNER // SSM_HEAD_DIM
SSM_GROUPS = 4
SSM_STATE = 128
CONV_WIDTH = 4
SSM_CHUNK = 128
CONV_DIM = SSM_INNER + 2 * SSM_GROUPS * SSM_STATE
D_FF = 2816
FFN_RES = 0.5
N_SUB = 3
EPS = 1e-6
IN_SIZES = (Q_LORA, KV_LORA, QK_ROPE, SSM_INNER, CONV_DIM, SSM_HEADS, D_MODEL, D_MODEL)
IN_OFFSETS = tuple(int(v) for v in np.cumsum((0,) + IN_SIZES))

LANES = 128
SUBLANES = 8
VMEM_LIMIT_BYTES = 56 * 1024 * 1024

PAGED_SLOTS = 4
PAGED_AHEAD = 2
HEAD_PAD = LANES
HEADS_PER_GROUP = SSM_HEADS // SSM_GROUPS
GROUP_CH = HEADS_PER_GROUP * SSM_HEAD_DIM
SMALL_W = Q_LORA + KV_LORA + 2 * LANES
KROPE_CHUNK = (Q_LORA + KV_LORA) // LANES
DT_CHUNK = KROPE_CHUNK + 1

BF16 = jnp.bfloat16
F32 = jnp.float32
_NT = (((1,), (1,)), ((), ()))


def _tile(n, target):
    t = min(n, target)
    while n % t:
        t -= 1
    return t


def _params(*sem):
    return pltpu.CompilerParams(dimension_semantics=sem, vmem_limit_bytes=VMEM_LIMIT_BYTES)


def _rms(x, g):
    return x * lax.rsqrt(jnp.mean(x * x, axis=-1, keepdims=True) + EPS) * g


def _silu(x):
    return x * jax.nn.sigmoid(x)


def _dot(a, b):
    return jnp.dot(a, b, preferred_element_type=F32)


def _split2_dot(v, m):
    hi = v.astype(BF16)
    mid = (v - hi.astype(F32)).astype(BF16)
    return _dot(hi, m) + _dot(mid, m)


def _split2_dot_stacked(vs, m):
    rows = vs[0].shape[0]
    parts = []
    for v in vs:
        hi = v.astype(BF16)
        parts += [hi, (v - hi.astype(F32)).astype(BF16)]
    out = _dot(jnp.concatenate(parts, axis=0), m)
    return [out[2 * i * rows:(2 * i + 1) * rows] + out[(2 * i + 1) * rows:(2 * i + 2) * rows]
            for i in range(len(vs))]


def _split3_dot(m, v):
    hi = v.astype(BF16)
    r1 = v - hi.astype(F32)
    mid = r1.astype(BF16)
    lo = (r1 - mid.astype(F32)).astype(BF16)
    return _dot(m, hi) + _dot(m, mid) + _dot(m, lo)


def _mod_spec(mod, tm, rows_per_seq, ngrid):
    d = mod.shape[-1]
    if mod.ndim == 3:
        tiles = rows_per_seq // tm
        if ngrid == 1:
            return pl.BlockSpec((None, 1, d), lambda i: (i // tiles, 0, 0))
        return pl.BlockSpec((None, 1, d), lambda i, j: (i // tiles, 0, 0))
    if ngrid == 1:
        return pl.BlockSpec((tm, d), lambda i: (i, 0))
    return pl.BlockSpec((tm, d), lambda i, j: (i, 0))


def _row_spec(width, ngrid):
    if ngrid == 1:
        return pl.BlockSpec((1, width), lambda i: (0, 0))
    return pl.BlockSpec((1, width), lambda i, j: (0, 0))


def _ada_kernel(c_ref, w_ref, b_ref, o_ref):
    h = _silu(c_ref[...]).astype(BF16)
    o_ref[...] = _dot(h, w_ref[...].astype(BF16)) + b_ref[...]


def _ada(c, w, b):
    m, d = c.shape
    n = w.shape[1]
    tn = _tile(n, 1024)
    return pl.pallas_call(
        _ada_kernel,
        grid=(n // tn,),
        in_specs=[pl.BlockSpec((m, d), lambda j: (0, 0)),
                  pl.BlockSpec((d, tn), lambda j: (0, j)),
                  pl.BlockSpec((1, tn), lambda j: (0, j))],
        out_specs=pl.BlockSpec((m, tn), lambda j: (0, j)),
        out_shape=jax.ShapeDtypeStruct((m, n), F32),
        compiler_params=_params("arbitrary"),
    )(c, w, b.reshape(1, n))


def _ffn_kernel(x_ref, shift_ref, scale_ref, gate_ref, gpre_ref, gpost_ref, wg_ref, wu_ref, wd_ref,
                o_ref, h_scr, acc_scr):
    k = pl.program_id(1)

    @pl.when(k == 0)
    def _():
        h = _rms(x_ref[...], gpre_ref[...]) * (1.0 + scale_ref[...]) + shift_ref[...]
        h_scr[...] = h.astype(BF16)
        acc_scr[...] = jnp.zeros_like(acc_scr)

    h = h_scr[...]
    a = _dot(h, wg_ref[...])
    u = _dot(h, wu_ref[...])
    acc_scr[...] += _dot((_silu(a) * u).astype(BF16), wd_ref[...])

    @pl.when(k == pl.num_programs(1) - 1)
    def _():
        f = _rms(acc_scr[...], gpost_ref[...])
        o_ref[...] = x_ref[...] + FFN_RES * gate_ref[...] * f


def _ffn(x, shift, scale, gate, g_pre, g_post, wg, wu, wd, rows_per_seq):
    n, d = x.shape
    dff = wg.shape[1]
    tm = _tile(rows_per_seq if shift.ndim == 3 else n, 512)
    tf = _tile(dff, 1408)
    mspec = lambda m: _mod_spec(m, tm, rows_per_seq, 2)
    return pl.pallas_call(
        _ffn_kernel,
        grid=(n // tm, dff // tf),
        in_specs=[pl.BlockSpec((tm, d), lambda i, k: (i, 0)),
                  mspec(shift), mspec(scale), mspec(gate),
                  _row_spec(d, 2), _row_spec(d, 2),
                  pl.BlockSpec((d, tf), lambda i, k: (0, k)),
                  pl.BlockSpec((d, tf), lambda i, k: (0, k)),
                  pl.BlockSpec((tf, d), lambda i, k: (k, 0))],
        out_specs=pl.BlockSpec((tm, d), lambda i, k: (i, 0)),
        out_shape=jax.ShapeDtypeStruct((n, d), F32),
        scratch_shapes=[pltpu.VMEM((tm, d), BF16), pltpu.VMEM((tm, d), F32)],
        compiler_params=_params("parallel", "arbitrary"),
    )(x, shift, scale, gate, g_pre.reshape(1, d), g_post.reshape(1, d), wg, wu, wd)


def _proj_kernel(x_ref, shift_ref, scale_ref, gpre_ref, w_ref, o_ref, h_scr):
    @pl.when(pl.program_id(1) == 0)
    def _():
        h = _rms(x_ref[...], gpre_ref[...]) * (1.0 + scale_ref[...]) + shift_ref[...]
        h_scr[...] = h.astype(BF16)

    o_ref[...] = _dot(h_scr[...], w_ref[...]).astype(o_ref.dtype)


def _proj(x, shift, scale, g_pre, w, out_dtype, rows_per_seq):
    n, d = x.shape
    nout = w.shape[1]
    tm = _tile(rows_per_seq if shift.ndim == 3 else n, 1024)
    tn = _tile(nout, 1024)
    mspec = lambda m: _mod_spec(m, tm, rows_per_seq, 2)
    return pl.pallas_call(
        _proj_kernel,
        grid=(n // tm, nout // tn),
        in_specs=[pl.BlockSpec((tm, d), lambda i, j: (i, 0)),
                  mspec(shift), mspec(scale), _row_spec(d, 2),
                  pl.BlockSpec((d, tn), lambda i, j: (0, j))],
        out_specs=pl.BlockSpec((tm, tn), lambda i, j: (i, j)),
        out_shape=jax.ShapeDtypeStruct((n, nout), out_dtype),
        scratch_shapes=[pltpu.VMEM((tm, d), BF16)],
        compiler_params=_params("parallel", "arbitrary"),
    )(x, shift, scale, g_pre.reshape(1, d), w)


def _rope(x, cos, sin):
    w = x.shape[1]
    lane = lax.broadcasted_iota(jnp.int32, x.shape, 1) % LANES
    first_half = (lane >= QK_NOPE) & (lane < QK_NOPE + QK_ROPE // 2)
    swapped = jnp.where(first_half, pltpu.roll(x, w - QK_ROPE // 2, 1), pltpu.roll(x, QK_ROPE // 2, 1))
    return x * cos + swapped * sin


def _mla_kernel(prompt, small_ref, cos_ref, sin_ref, gq_ref, gkv_ref, wuq_ref, *rest):
    if prompt:
        wk_ref, wv_ref, q_ref, ckv_ref, kpe_ref, k_ref, v_ref = rest
    else:
        q_ref, ckv_ref, kpe_ref = rest
    sm = small_ref[...]
    cos, sin = cos_ref[...], sin_ref[...]
    qn = _rms(sm[:, :Q_LORA], gq_ref[...])
    q = _dot(qn.astype(BF16), wuq_ref[...])
    q = _rope(q, jnp.tile(cos, (1, MLA_HEADS)), jnp.tile(sin, (1, MLA_HEADS)))
    q_ref[...] = (q * Q_LOG2_SCALE).astype(BF16)
    ckv = _rms(sm[:, Q_LORA:Q_LORA + KV_LORA], gkv_ref[...])
    ckv_ref[...] = ckv
    kr = _rope(sm[:, KROPE_CHUNK * LANES:(KROPE_CHUNK + 1) * LANES], cos, sin)
    kpe_ref[...] = kr[:, QK_NOPE:QK_NOPE + QK_ROPE]
    if prompt:
        cb = ckv.astype(BF16)
        k_ref[...] = (_dot(cb, wk_ref[...]) + jnp.tile(kr, (1, MLA_HEADS))).astype(BF16)
        v_ref[...] = _dot(cb, wv_ref[...]).astype(BF16)


def _mla(small, cos, sin, g_q, g_kv, wuq, wk, wv, rows_per_seq, prompt):
    n = small.shape[0]
    hw = MLA_HEADS * HEAD_PAD
    tm = _tile(rows_per_seq if prompt else n, 512)
    tiles = rows_per_seq // tm if prompt else 1
    tab = pl.BlockSpec((tm, LANES), (lambda i: (i % tiles, 0)) if prompt else (lambda i: (0, 0)))
    const = lambda r, c: pl.BlockSpec((r, c), lambda i: (0, 0))
    rows = lambda c: pl.BlockSpec((tm, c), lambda i: (i, 0))
    in_specs = [rows(SMALL_W), tab, tab, const(1, Q_LORA), const(1, KV_LORA), const(Q_LORA, hw)]
    args = [small, cos, sin, g_q.reshape(1, -1), g_kv.reshape(1, -1), wuq]
    out_specs = [rows(hw), rows(KV_LORA), rows(QK_ROPE)]
    out_shape = [jax.ShapeDtypeStruct((n, hw), BF16), jax.ShapeDtypeStruct((n, KV_LORA), F32),
                 jax.ShapeDtypeStruct((n, QK_ROPE), F32)]
    if prompt:
        in_specs += [const(KV_LORA, hw), const(KV_LORA, hw)]
        args += [wk, wv]
        out_specs += [rows(hw), rows(hw)]
        out_shape += [jax.ShapeDtypeStruct((n, hw), BF16)] * 2
    return pl.pallas_call(
        functools.partial(_mla_kernel, prompt),
        grid=(n // tm,),
        in_specs=in_specs, out_specs=out_specs, out_shape=out_shape,
        compiler_params=_params("parallel"),
    )(*args)


def _attn_kernel(tq, q_ref, k_ref, v_ref, o_ref):
    j = pl.program_id(2)
    causal = (lax.broadcasted_iota(jnp.int32, (tq, tq), 1) <= lax.broadcasted_iota(jnp.int32, (tq, tq), 0))

    def one_tile(parity):
        q = q_ref[parity * tq:(parity + 1) * tq, :]

        def scores(kb):
            k = k_ref[pl.ds(pl.multiple_of(kb * tq, tq), tq), :]
            return lax.dot_general(q, k, _NT, preferred_element_type=F32)

        def fold(state, s, kb):
            m, l, acc = state
            v = v_ref[pl.ds(pl.multiple_of(kb * tq, tq), tq), :]
            m_new = jnp.maximum(m, jnp.max(s, axis=-1, keepdims=True))
            alpha = jnp.exp2(m - m_new)
            p = jnp.exp2(s - m_new)
            l = alpha * l + jnp.sum(p, axis=-1, keepdims=True)
            return m_new, l, alpha * acc + _dot(p.astype(BF16), v)

        def pair(i, state):
            s0, s1 = scores(2 * i), scores(2 * i + 1)
            return fold(fold(state, s0, 2 * i), s1, 2 * i + 1)

        init = (jnp.full((tq, 1), -jnp.inf, F32), jnp.zeros((tq, 1), F32), jnp.zeros((tq, HEAD_PAD), F32))
        state = lax.fori_loop(0, j, pair, init)
        diag = 2 * j + parity
        if parity:
            s_full, s_diag = scores(2 * j), scores(diag)
            state = fold(state, s_full, 2 * j)
        else:
            s_diag = scores(diag)
        _, l, acc = fold(state, jnp.where(causal, s_diag, -jnp.inf), diag)
        o_ref[parity * tq:(parity + 1) * tq, :] = (acc / l).astype(BF16)

    for parity in range(q_ref.shape[0] // tq):
        one_tile(parity)


def _prompt_attention(q, k, v, bsz, t):
    assert t % 2 == 0, t
    tq = _tile(t // 2, 512)
    nq = t // (2 * tq)
    return pl.pallas_call(
        functools.partial(_attn_kernel, tq),
        grid=(bsz, MLA_HEADS, nq),
        in_specs=[pl.BlockSpec((2 * tq, HEAD_PAD), lambda b, h, i: (b * nq + i, h)),
                  pl.BlockSpec((t, HEAD_PAD), lambda b, h, i: (b, h)),
                  pl.BlockSpec((t, HEAD_PAD), lambda b, h, i: (b, h))],
        out_specs=pl.BlockSpec((2 * tq, HEAD_PAD), lambda b, h, i: (b * nq + i, h)),
        out_shape=jax.ShapeDtypeStruct(q.shape, BF16),
        compiler_params=_params("parallel", "parallel", "arbitrary"),
    )(q, k, v)


def _head_mm_kernel(x_ref, w_ref, o_ref):
    o_ref[...] = _dot(x_ref[...], w_ref[...]).astype(o_ref.dtype)


def _head_mm(x, w):
    n = x.shape[0]
    nh, kin, kout = w.shape
    return pl.pallas_call(
        _head_mm_kernel,
        grid=(nh,),
        in_specs=[pl.BlockSpec((n, kin), lambda h: (0, h)),
                  pl.BlockSpec((None, kin, kout), lambda h: (h, 0, 0))],
        out_specs=pl.BlockSpec((n, kout), lambda h: (0, h)),
        out_shape=jax.ShapeDtypeStruct((n, nh * kout), BF16),
        compiler_params=_params("parallel"),
    )(x, w)


def _paged_kernel(pp, sub, per_seq, n_chunks, t_new, pt_ref, q_ref, cnew_ref, penew_ref, lat_hbm, pe_hbm,
                  o_ref, lat_buf, pe_buf, sems):
    b = pl.program_id(0)
    rows = q_ref.shape[0]

    def copies(chunk, slot, i):
        page = pt_ref[chunk * pp + i]
        return (pltpu.make_async_copy(lat_hbm.at[page], lat_buf.at[slot, i], sems.at[0, slot]),
                pltpu.make_async_copy(pe_hbm.at[page], pe_buf.at[slot, i], sems.at[1, slot]))

    def fetch(chunk, slot):
        for i in range(pp):
            lat_cp, pe_cp = copies(chunk, slot, i)
            lat_cp.start(priority=i % 2)
            pe_cp.start(priority=(i + 1) % 2)

    def wait(chunk, slot):
        for i in range(pp):
            for cp in copies(chunk, slot, i):
                cp.wait()

    @pl.when(b == 0)
    def _():
        for ahead in range(PAGED_AHEAD):
            fetch(ahead, ahead)

    q = q_ref[...]
    q_abs = q[:, :KV_LORA]
    q_pe = q[:, KV_LORA:KV_LORA + QK_ROPE]

    def fold(state, s, c):
        m_old, l_old, acc = state
        m_new = jnp.maximum(m_old, jnp.max(s, axis=-1, keepdims=True))
        alpha = jnp.exp2(m_old - m_new)
        p = jnp.exp2(s - m_new)
        l_new = alpha * l_old + jnp.sum(p, axis=-1, keepdims=True)
        return m_new, l_new, alpha * acc + _dot(p.astype(BF16), c)

    def scores(slot, g0):
        c = jnp.concatenate([lat_buf[slot, i].astype(BF16) for i in range(g0, g0 + sub)], axis=0)
        pe_t = jnp.concatenate([pe_buf[slot, i].astype(BF16) for i in range(g0, g0 + sub)], axis=1)
        return lax.dot_general(q_abs, c, _NT, preferred_element_type=F32) + _dot(q_pe, pe_t), c

    def consume(slot, state):
        pending = scores(slot, 0)
        for g0 in range(sub, pp, sub):
            upcoming = scores(slot, g0)
            state = fold(state, *pending)
            pending = upcoming
        return fold(state, *pending)

    def ring_turn(jj, state):
        for slot in range(PAGED_SLOTS):
            chunk = b * per_seq + PAGED_SLOTS * jj + slot

            @pl.when(chunk + PAGED_AHEAD < n_chunks)
            def _():
                fetch(chunk + PAGED_AHEAD, (slot + PAGED_AHEAD) % PAGED_SLOTS)

            wait(chunk, slot)
            state = consume(slot, state)
        return state

    init = (jnp.full((rows, 1), -jnp.inf, F32), jnp.zeros((rows, 1), F32), jnp.zeros((rows, KV_LORA), F32))
    state = lax.fori_loop(0, per_seq // PAGED_SLOTS, ring_turn, init)

    pad = LANES - t_new
    c = jnp.concatenate([cnew_ref[...], jnp.zeros((pad, KV_LORA), F32)], axis=0).astype(BF16)
    pe = jnp.concatenate([penew_ref[...], jnp.zeros((pad, QK_ROPE), F32)], axis=0).astype(BF16)
    s_new = (lax.dot_general(q_abs, c, _NT, preferred_element_type=F32)
             + lax.dot_general(q_pe, pe, _NT, preferred_element_type=F32))
    t_row = lax.broadcasted_iota(jnp.int32, (rows, LANES), 0) // MLA_HEADS
    key = lax.broadcasted_iota(jnp.int32, (rows, LANES), 1)
    _, l, acc = fold(state, jnp.where(key <= t_row, s_new, -jnp.inf), c)
    o_ref[...] = (acc / l).astype(BF16)


def _paged_attention(q, c_new, pe_new, cache_c, cache_pe_t, page_table):
    bsz, rows, qw = q.shape
    t_new = c_new.shape[1]
    n_pages = page_table.shape[1]
    page = cache_c.shape[1]
    assert n_pages % PAGED_SLOTS == 0, n_pages
    pp = _tile(n_pages // PAGED_SLOTS, 16)
    sub = _tile(pp, 4)
    per_seq = n_pages // pp
    seq = lambda w0, w1: pl.BlockSpec((None, w0, w1), lambda b, pt: (b, 0, 0))
    grid_spec = pltpu.PrefetchScalarGridSpec(
        num_scalar_prefetch=1,
        grid=(bsz,),
        in_specs=[seq(rows, qw), seq(t_new, KV_LORA), seq(t_new, QK_ROPE),
                  pl.BlockSpec(memory_space=pl.ANY), pl.BlockSpec(memory_space=pl.ANY)],
        out_specs=seq(rows, KV_LORA),
        scratch_shapes=[pltpu.VMEM((PAGED_SLOTS, pp, page, KV_LORA), F32),
                        pltpu.VMEM((PAGED_SLOTS, pp, QK_ROPE, page), F32),
                        pltpu.SemaphoreType.DMA((2, PAGED_SLOTS))])
    return pl.pallas_call(
        functools.partial(_paged_kernel, pp, sub, per_seq, bsz * per_seq, t_new),
        grid_spec=grid_spec,
        out_shape=jax.ShapeDtypeStruct((bsz, rows, KV_LORA), BF16),
        compiler_params=_params("arbitrary"),
    )(page_table.reshape(-1), q, c_new, pe_new, cache_c, cache_pe_t)


def _ssd_kernel(lb, nb, xs_ref, b_ref, c_ref, dt_ref, wxs_ref, wb_ref, wc_ref, bxs_ref, bb_ref, bc_ref,
                dtb_ref, alog_ref, dsk_ref, txs_ref, tb_ref, tc_ref, h0_ref,
                y_ref, hout_ref, txs_scr, tb_scr, tc_scr, h_scr):
    ci = pl.program_id(2)

    @pl.when(ci == 0)
    def _():
        txs_scr[...] = txs_ref[...]
        tb_scr[...] = tb_ref[...]
        tc_scr[...] = tc_ref[...]
        h_scr[...] = h0_ref[...].reshape(h_scr.shape)

    for s in range(nb):
        _ssd_chunk(lb, s, xs_ref, b_ref, c_ref, dt_ref, wxs_ref, wb_ref, wc_ref, bxs_ref, bb_ref, bc_ref,
                   dtb_ref, alog_ref, dsk_ref, y_ref, txs_scr, tb_scr, tc_scr, h_scr)

    @pl.when(ci == pl.num_programs(2) - 1)
    def _():
        hout_ref[...] = h_scr[...].reshape(hout_ref.shape)


def _ssd_chunk(lb, s, xs_ref, b_ref, c_ref, dt_ref, wxs_ref, wb_ref, wc_ref, bxs_ref, bb_ref, bc_ref,
               dtb_ref, alog_ref, dsk_ref, y_ref, txs_scr, tb_scr, tc_scr, h_scr):
    L = max(lb, 2 * SUBLANES)
    g = pl.program_id(1)

    def pad_rows(x):
        if lb == L:
            return x
        return jnp.concatenate([x, jnp.zeros((L - lb, x.shape[1]), x.dtype)], axis=0)

    def conv_silu(x_ref, tail_scr, w_ref, bias_ref):
        x = pad_rows(x_ref[s].astype(F32))
        tail = tail_scr[s]
        w = w_ref[...]
        row = lax.broadcasted_iota(jnp.int32, (SUBLANES, x.shape[1]), 0)
        out = bias_ref[...] + w[CONV_WIDTH - 1:CONV_WIDTH] * x
        for k in range(1, CONV_WIDTH):
            sh = pltpu.roll(x, k, 0)
            head = jnp.where(row < k, pltpu.roll(tail, k, 0), sh[:SUBLANES])
            sh = jnp.concatenate([head, sh[SUBLANES:]], axis=0)
            out = out + w[CONV_WIDTH - 1 - k:CONV_WIDTH - k] * sh
        tail_scr[s] = x[lb - SUBLANES:lb]
        return _silu(out)

    xs = conv_silu(xs_ref, txs_scr, wxs_ref, bxs_ref)
    bm = conv_silu(b_ref, tb_scr, wb_ref, bb_ref)
    cm = conv_silu(c_ref, tc_scr, wc_ref, bc_ref)

    shift = (LANES - g * HEADS_PER_GROUP) % LANES
    dt_raw = pltpu.roll(pad_rows(dt_ref[s]) + dtb_ref[...], shift, 1)
    a = -jnp.exp(pltpu.roll(alog_ref[...], shift, 1))
    dt = jax.nn.softplus(dt_raw)
    if lb < L:
        dt = jnp.where(lax.broadcasted_iota(jnp.int32, dt.shape, 0) < lb, dt, 0.0)
    la = dt * a

    r_i = lax.broadcasted_iota(jnp.int32, (L, L), 0)
    c_i = lax.broadcasted_iota(jnp.int32, (L, L), 1)
    tril = r_i >= c_i
    cum = _split3_dot(tril.astype(BF16), la)
    cum_t = cum.T
    ecum = jnp.exp(cum)
    to_end = jnp.exp(cum[L - 1:L] - cum)

    e_r = lax.broadcasted_iota(jnp.int32, (LANES, GROUP_CH), 0)
    e_c = lax.broadcasted_iota(jnp.int32, (LANES, GROUP_CH), 1) // SSM_HEAD_DIM
    expand = (e_r == e_c).astype(BF16)
    dt_e, ecum_e, to_end_e = _split2_dot_stacked([dt, ecum, to_end], expand)

    xdt = xs * dt_e
    bmb, cmb = bm.astype(BF16), cm.astype(BF16)
    cb = lax.dot_general(cmb, bmb, _NT, preferred_element_type=F32)
    half = lax.broadcasted_iota(jnp.int32, (L, LANES), 1) // SSM_HEAD_DIM
    pairs = []
    for pr in range(HEADS_PER_GROUP // 2):
        xp = xdt[:, pr * LANES:(pr + 1) * LANES]
        acc = jnp.zeros((L, LANES), F32)
        for hh in range(2):
            h = 2 * pr + hh
            seg = cum[:, h:h + 1] - cum_t[h:h + 1, :]
            w = (cb * jnp.exp(jnp.where(tril, seg, -jnp.inf))).astype(BF16)
            acc += _dot(w, jnp.where(half == hh, xp, 0.0).astype(BF16))
        pairs.append(acc)
    y_diag = jnp.concatenate(pairs, axis=1)

    h_prev = h_scr[s]
    y_off = lax.dot_general(cmb, h_prev.astype(BF16), _NT, preferred_element_type=F32) * ecum_e
    y = y_diag + y_off + dsk_ref[...] * xs
    y_ref[s] = y[:lb].astype(y_ref.dtype)

    xw_t = (xdt * to_end_e).T.astype(BF16)
    upd = _dot(xw_t, bmb)
    d_last = jnp.exp(cum_t[:, L - 1:L])
    dec = jnp.concatenate([jnp.broadcast_to(d_last[h:h + 1, :], (SSM_HEAD_DIM, SSM_STATE))
                           for h in range(HEADS_PER_GROUP)], axis=0)
    h_scr[s] = h_prev * dec + upd


def _ssd(xbc, small, conv_w, conv_b, dt_bias, a_log, d_skip, conv_tail, h0, bsz, t):
    lb = min(t, SSM_CHUNK)
    nc = t // lb
    gs, hpg = SSM_GROUPS, HEADS_PER_GROUP
    b_blk0 = SSM_INNER // SSM_STATE
    c_blk0 = b_blk0 + gs
    pad128 = lambda v: jnp.pad(v, (0, LANES - v.shape[0])).reshape(1, LANES)
    dsk = jnp.repeat(d_skip, SSM_HEAD_DIM).reshape(1, SSM_INNER)
    cbias = conv_b.reshape(1, CONV_DIM)

    nb = _tile(bsz, 8)
    seq3 = lambda w, col: pl.BlockSpec((nb, lb, w), lambda b, g, c: (b, c, col(g)))
    par2 = lambda r, w, col: pl.BlockSpec((r, w), lambda b, g, c: (0, col(g)))
    tail3 = lambda w, col: pl.BlockSpec((nb, SUBLANES, w), lambda b, g, c: (b, 0, col(g)))
    xs_col = lambda g: g
    b_col = lambda g: b_blk0 + g
    c_col = lambda g: c_blk0 + g
    zero = lambda g: 0
    state_spec = pl.BlockSpec((nb, hpg, SSM_HEAD_DIM, SSM_STATE), lambda b, g, c: (b, g, 0, 0))
    y, h_out = pl.pallas_call(
        functools.partial(_ssd_kernel, lb, nb),
        grid=(bsz // nb, gs, nc),
        in_specs=[seq3(GROUP_CH, xs_col), seq3(SSM_STATE, b_col), seq3(SSM_STATE, c_col),
                  seq3(LANES, lambda g: DT_CHUNK),
                  par2(CONV_WIDTH, GROUP_CH, xs_col), par2(CONV_WIDTH, SSM_STATE, b_col),
                  par2(CONV_WIDTH, SSM_STATE, c_col),
                  par2(1, GROUP_CH, xs_col), par2(1, SSM_STATE, b_col), par2(1, SSM_STATE, c_col),
                  par2(1, LANES, zero), par2(1, LANES, zero), par2(1, GROUP_CH, xs_col),
                  tail3(GROUP_CH, xs_col), tail3(SSM_STATE, b_col), tail3(SSM_STATE, c_col),
                  state_spec],
        out_specs=[seq3(GROUP_CH, xs_col), state_spec],
        out_shape=[jax.ShapeDtypeStruct((bsz, t, SSM_INNER), BF16),
                   jax.ShapeDtypeStruct(h0.shape, F32)],
        scratch_shapes=[pltpu.VMEM((nb, SUBLANES, GROUP_CH), F32), pltpu.VMEM((nb, SUBLANES, SSM_STATE), F32),
                        pltpu.VMEM((nb, SUBLANES, SSM_STATE), F32),
                        pltpu.VMEM((nb, GROUP_CH, SSM_STATE), F32)],
        compiler_params=_params("parallel", "parallel", "arbitrary"),
    )(xbc, xbc, xbc, small, conv_w, conv_w, conv_w, cbias, cbias, cbias,
      pad128(dt_bias), pad128(a_log), dsk, conv_tail, conv_tail, conv_tail, h0)
    return y, h_out


def _mix_kernel(x_ref, attn_ref, y_ref, z_ref, gates_ref, gate_ref, gssm_ref, gpost_ref,
                woa_ref, wos_ref, wout_ref, o_ref):
    o_attn = _dot(attn_ref[...], woa_ref[...])
    yz = y_ref[...].astype(F32) * _silu(z_ref[...].astype(F32))
    o_ssm = _dot(_rms(yz, gssm_ref[...]).astype(BF16), wos_ref[...])
    gates = gates_ref[...].astype(F32)
    mixed = (jax.nn.sigmoid(gates[:, :D_MODEL]) * o_attn
             + jax.nn.sigmoid(gates[:, D_MODEL:]) * o_ssm)
    m2 = _dot(mixed.astype(BF16), wout_ref[...])
    o_ref[...] = x_ref[...] + gate_ref[...] * _rms(m2, gpost_ref[...])


def _mix(x, attn, y, zg, gate, g_ssm, g_post, woa, wos, wout, rows_per_seq):
    n, d = x.shape
    tm = _tile(rows_per_seq if gate.ndim == 3 else n, 256)
    rows = lambda w, col: pl.BlockSpec((tm, w), lambda i: (i, col))
    const = lambda r, c: pl.BlockSpec((r, c), lambda i: (0, 0))
    return pl.pallas_call(
        _mix_kernel,
        grid=(n // tm,),
        in_specs=[rows(d, 0), rows(attn.shape[1], 0), rows(SSM_INNER, 0), rows(SSM_INNER, 0),
                  rows(2 * d, 1), _mod_spec(gate, tm, rows_per_seq, 1),
                  const(1, SSM_INNER), const(1, d),
                  const(*woa.shape), const(*wos.shape), const(*wout.shape)],
        out_specs=rows(d, 0),
        out_shape=jax.ShapeDtypeStruct((n, d), F32),
        compiler_params=_params("parallel"),
    )(x, attn, y, zg, zg, gate, g_ssm.reshape(1, -1), g_post.reshape(1, -1), woa, wos, wout)


def _prep_weights(w_in, w_uq, w_ukv, w_o_attn):
    o = IN_OFFSETS
    seg = lambda i: w_in[:, o[i]:o[i + 1]]
    zeros = lambda c: jnp.zeros((D_MODEL, c), w_in.dtype)
    w_small = jnp.concatenate([seg(0), seg(1), zeros(QK_NOPE), seg(2), zeros(LANES - QK_NOPE - QK_ROPE),
                               seg(5), zeros(LANES - SSM_HEADS)], axis=1)
    w_xbc = seg(4)
    w_zg = jnp.concatenate([seg(3), seg(6), seg(7)], axis=1)
    hd = QK_NOPE + QK_ROPE
    wuq = jnp.pad(w_uq.reshape(Q_LORA, MLA_HEADS, hd), ((0, 0), (0, 0), (0, HEAD_PAD - hd)))
    wuq = wuq.reshape(Q_LORA, MLA_HEADS * HEAD_PAD)
    wkv = w_ukv.reshape(KV_LORA, MLA_HEADS, QK_NOPE + V_HEAD)
    w_uk, w_uv = wkv[..., :QK_NOPE], wkv[..., QK_NOPE:]
    pad_h = lambda w, width: jnp.pad(w, ((0, 0), (0, 0), (0, HEAD_PAD - width)))
    wk = pad_h(w_uk, QK_NOPE).reshape(KV_LORA, MLA_HEADS * HEAD_PAD)
    wv = pad_h(w_uv, V_HEAD).reshape(KV_LORA, MLA_HEADS * HEAD_PAD)
    w_abs = jnp.zeros((MLA_HEADS, HEAD_PAD, KV_LORA + LANES), w_ukv.dtype)
    w_abs = w_abs.at[:, :QK_NOPE, :KV_LORA].set(jnp.transpose(w_uk, (1, 2, 0)))
    w_abs = w_abs.at[:, QK_NOPE:hd, KV_LORA:KV_LORA + QK_ROPE].set(jnp.eye(QK_ROPE, dtype=w_ukv.dtype))
    w_uv_h = pad_h(jnp.transpose(w_uv, (1, 0, 2)), V_HEAD)
    woa = jnp.pad(w_o_attn.reshape(MLA_HEADS, V_HEAD, D_MODEL), ((0, 0), (0, HEAD_PAD - V_HEAD), (0, 0)))
    woa = woa.reshape(MLA_HEADS * HEAD_PAD, D_MODEL)
    cast = lambda w: w.astype(BF16)
    return tuple(map(cast, (w_small, w_xbc, w_zg, wuq, wk, wv, w_abs, w_uv_h, woa)))


def _rope_tables(pos):
    half = QK_ROPE // 2
    inv = ROPE_THETA ** (-jnp.arange(half, dtype=F32) / half)
    ang = pos.astype(F32)[:, None] * inv[None, :]
    cos, sin = jnp.cos(ang), jnp.sin(ang)
    n = pos.shape[0]
    ones, zeros = jnp.ones((n, QK_NOPE), F32), jnp.zeros((n, QK_NOPE), F32)
    tail = jnp.zeros((n, LANES - QK_NOPE - QK_ROPE), F32)
    return (jnp.concatenate([ones, cos, cos, tail], axis=1),
            jnp.concatenate([zeros, -sin, sin, tail], axis=1))


def _layer(x, mods, pos, per_row, conv_prev, ssm_prev, paged, lw):
    (g_pre, g_post, wg, wu, wd, w_small, w_xbc, w_zg, g_q, wuq, g_kv, wk, wv, w_abs, w_uv_h, woa,
     conv_w, conv_b, dt_bias, a_log, d_skip, g_ssm, wos, wout) = lw
    bsz, t, d = x.shape
    n = bsz * t
    x2 = x.reshape(n, d)
    if per_row:
        mod = lambda s, k: jnp.repeat(mods[:, s, k], t, axis=0)
    else:
        mod = lambda s, k: mods[:, s, k][:, None, :]

    x2 = _ffn(x2, mod(0, 0), mod(0, 1), mod(0, 2), g_pre[0], g_post[0], wg[0], wu[0], wd[0], t)

    sh, sc = mod(1, 0), mod(1, 1)
    small = _proj(x2, sh, sc, g_pre[1], w_small, F32, t)
    xbc = _proj(x2, sh, sc, g_pre[1], w_xbc, BF16, t)
    zg = _proj(x2, sh, sc, g_pre[1], w_zg, BF16, t)

    cos, sin = _rope_tables(pos)
    if per_row:
        tm = _tile(n, 512)
        cos, sin = jnp.tile(cos, (tm // t, 1)), jnp.tile(sin, (tm // t, 1))
    mla = _mla(small, cos, sin, g_q, g_kv, wuq, wk, wv, t, paged is None)
    if paged is None:
        q, c_kv, k_pe, k, v = mla
        attn = _prompt_attention(q, k, v, bsz, t)
    else:
        q, c_kv, k_pe = mla
        cache_c, cache_pe, page_table = paged
        q_full = _head_mm(q, w_abs).reshape(bsz, t * MLA_HEADS, KV_LORA + LANES)
        o_lat = _paged_attention(q_full, c_kv.reshape(bsz, t, KV_LORA), k_pe.reshape(bsz, t, QK_ROPE),
                                 cache_c, cache_pe, page_table)
        attn = _head_mm(o_lat.reshape(n, MLA_HEADS * KV_LORA), w_uv_h)

    xbc3 = xbc.reshape(bsz, t, CONV_DIM)
    tail = jnp.pad(conv_prev.astype(F32), ((0, 0), (SUBLANES - (CONV_WIDTH - 1), 0), (0, 0)))
    y, new_ssm = _ssd(xbc3, small.reshape(bsz, t, SMALL_W), conv_w, conv_b, dt_bias, a_log, d_skip,
                      tail, ssm_prev, bsz, t)
    u_tail = jnp.concatenate([conv_prev.astype(F32), xbc3[:, -(CONV_WIDTH - 1):].astype(F32)], axis=1)
    new_conv = u_tail[:, -(CONV_WIDTH - 1):]

    x2 = _mix(x2, attn, y.reshape(n, SSM_INNER), zg, mod(1, 2), g_ssm, g_post[1], woa, wos, wout, t)
    x2 = _ffn(x2, mod(2, 0), mod(2, 1), mod(2, 2), g_pre[2], g_post[2], wg[1], wu[1], wd[1], t)
    return (x2.reshape(bsz, t, d), c_kv.reshape(bsz, t, KV_LORA), k_pe.reshape(bsz, t, QK_ROPE),
            new_conv, new_ssm)


def kernel(x_prompt, x_sample, cache_kv_latent, cache_k_rope, state_conv, state_ssm, page_table,
           c_prompt, c_sample, w_ada, b_ada, g_pre, g_post, w_ffn_gate, w_ffn_up, w_ffn_down, w_in,
           g_q_lat, w_uq, g_kv_lat, w_ukv, w_o_attn, conv_w, conv_b, dt_bias, a_log, d_skip,
           g_ssm_norm, w_o_ssm, w_out):
    bp, t_prompt = x_prompt.shape[:2]
    bs, t_sample = x_sample.shape[:2]
    depth = w_in.shape[0]
    past_len = page_table.shape[1] * cache_kv_latent.shape[2]
    pos_p = jnp.arange(t_prompt, dtype=jnp.int32)
    pos_s = past_len + jnp.arange(t_sample, dtype=jnp.int32)
    yp, ys = x_prompt, x_sample
    outs = [[] for _ in range(8)]
    pad_c = (-(bp + bs)) % SUBLANES
    c_all = jnp.concatenate([c_prompt, c_sample, jnp.zeros((pad_c, D_MODEL), c_prompt.dtype)], axis=0)
    for l in range(depth):
        mods = _ada(c_all, w_ada[l], b_ada[l]).reshape(-1, N_SUB, 3, D_MODEL)
        prepped = _prep_weights(w_in[l], w_uq[l], w_ukv[l], w_o_attn[l])
        w_small, w_xbc, w_zg, wuq, wk, wv, w_abs, w_uv_h, woa = prepped
        cast = lambda w: w.astype(BF16)
        lw = (g_pre[l], g_post[l], cast(w_ffn_gate[l]), cast(w_ffn_up[l]), cast(w_ffn_down[l]),
              w_small, w_xbc, w_zg, g_q_lat[l], wuq, g_kv_lat[l], wk, wv, w_abs, w_uv_h, woa,
              conv_w[l], conv_b[l], dt_bias[l], a_log[l], d_skip[l], g_ssm_norm[l],
              cast(w_o_ssm[l]), cast(w_out[l]))
        conv0 = jnp.zeros((bp, CONV_WIDTH - 1, CONV_DIM), x_prompt.dtype)
        ssm0 = jnp.zeros((bp, SSM_HEADS, SSM_HEAD_DIM, SSM_STATE), state_ssm.dtype)
        res_p = _layer(yp, mods[:bp], pos_p, False, conv0, ssm0, None, lw)
        paged = (cache_kv_latent[l], jnp.swapaxes(cache_k_rope[l], 1, 2), page_table)
        res_s = _layer(ys, mods[bp:bp + bs], pos_s, True, state_conv[l], state_ssm[l], paged, lw)
        yp, ys = res_p[0], res_s[0]
        for i in range(4):
            outs[i].append(res_p[1 + i])
            outs[4 + i].append(res_s[1 + i])
    return (yp, ys) + tuple(jnp.stack(o) for o in outs)
```

```python
import functools

import jax
import jax.numpy as jnp
import numpy as np
from jax import lax
from jax.experimental import pallas as pl
from jax.experimental.pallas import tpu as pltpu

D_MODEL = 1024
MLA_HEADS = 16
QK_NOPE = 64
QK_ROPE = 32
V_HEAD = 64
Q_LORA = 512
KV_LORA = 256
ROPE_THETA = 10000.0
ATTN_SCALE = (QK_NOPE + QK_ROPE) ** -0.5
Q_LOG2_SCALE = ATTN_SCALE * float(np.log2(np.e))
SSM_INNER = 2 * D_MODEL
SSM_HEAD_DIM = 64
SSM_HEADS = SSM_INNER // SSM_HEAD_DIM
SSM_GROUPS = 4
SSM_STATE = 128
CONV_WIDTH = 4
SSM_CHUNK = 128
CONV_DIM = SSM_INNER + 2 * SSM_GROUPS * SSM_STATE
D_FF = 2816
FFN_RES = 0.5
N_SUB = 3
EPS = 1e-6
IN_SIZES = (Q_LORA, KV_LORA, QK_ROPE, SSM_INNER, CONV_DIM, SSM_HEADS, D_MODEL, D_MODEL)
IN_OFFSETS = tuple(int(v) for v in np.cumsum((0,) + IN_SIZES))

LANES = 128
SUBLANES = 8
VMEM_LIMIT_BYTES = 56 * 1024 * 1024

PAGED_SLOTS = 4
PAGED_AHEAD = 2
HEAD_PAD = LANES
HEADS_PER_GROUP = SSM_HEADS // SSM_GROUPS
GROUP_CH = HEADS_PER_GROUP * SSM_HEAD_DIM
SMALL_W = Q_LORA + KV_LORA + 2 * LANES
KROPE_CHUNK = (Q_LORA + KV_LORA) // LANES
DT_CHUNK = KROPE_CHUNK + 1

BF16 = jnp.bfloat16
F32 = jnp.float32
_NT = (((1,), (1,)), ((), ()))


def _tile(n, target):
    t = min(n, target)
    while n % t:
        t -= 1
    return t


def _params(*sem):
    return pltpu.CompilerParams(dimension_semantics=sem, vmem_limit_bytes=VMEM_LIMIT_BYTES)


def _rms(x, g):
    return x * lax.rsqrt(jnp.mean(x * x, axis=-1, keepdims=True) + EPS) * g


def _silu(x):
    return x * jax.nn.sigmoid(x)


def _dot(a, b):
    return jnp.dot(a, b, preferred_element_type=F32)


def _split2_dot(v, m):
    hi = v.astype(BF16)
    mid = (v - hi.astype(F32)).astype(BF16)
    return _dot(hi, m) + _dot(mid, m)


def _split2_dot_stacked(vs, m):
    rows = vs[0].shape[0]
    parts = []
    for v in vs:
        hi = v.astype(BF16)
        parts += [hi, (v - hi.astype(F32)).astype(BF16)]
    out = _dot(jnp.concatenate(parts, axis=0), m)
    return [out[2 * i * rows:(2 * i + 1) * rows] + out[(2 * i + 1) * rows:(2 * i + 2) * rows]
            for i in range(len(vs))]


def _split3_dot(m, v):
    hi = v.astype(BF16)
    r1 = v - hi.astype(F32)
    mid = r1.astype(BF16)
    lo = (r1 - mid.astype(F32)).astype(BF16)
    return _dot(m, hi) + _dot(m, mid) + _dot(m, lo)


def _mod_spec(mod, tm, rows_per_seq, ngrid):
    d = mod.shape[-1]
    if mod.ndim == 3:
        tiles = rows_per_seq // tm
        if ngrid == 1:
            return pl.BlockSpec((None, 1, d), lambda i: (i // tiles, 0, 0))
        return pl.BlockSpec((None, 1, d), lambda i, j: (i // tiles, 0, 0))
    if ngrid == 1:
        return pl.BlockSpec((tm, d), lambda i: (i, 0))
    return pl.BlockSpec((tm, d), lambda i, j: (i, 0))


def _row_spec(width, ngrid):
    if ngrid == 1:
        return pl.BlockSpec((1, width), lambda i: (0, 0))
    return pl.BlockSpec((1, width), lambda i, j: (0, 0))


def _ada_kernel(c_ref, w_ref, b_ref, o_ref):
    h = _silu(c_ref[...]).astype(BF16)
    o_ref[...] = _dot(h, w_ref[...].astype(BF16)) + b_ref[...]


def _ada(c, w, b):
    m, d = c.shape
    n = w.shape[1]
    tn = _tile(n, 1024)
    return pl.pallas_call(
        _ada_kernel,
        grid=(n // tn,),
        in_specs=[pl.BlockSpec((m, d), lambda j: (0, 0)),
                  pl.BlockSpec((d, tn), lambda j: (0, j)),
                  pl.BlockSpec((1, tn), lambda j: (0, j))],
        out_specs=pl.BlockSpec((m, tn), lambda j: (0, j)),
        out_shape=jax.ShapeDtypeStruct((m, n), F32),
        compiler_params=_params("arbitrary"),
    )(c, w, b.reshape(1, n))


def _ffn_kernel(x_ref, shift_ref, scale_ref, gate_ref, gpre_ref, gpost_ref, wg_ref, wu_ref, wd_ref,
                o_ref, h_scr, acc_scr):
    k = pl.program_id(1)

    @pl.when(k == 0)
    def _():
        h = _rms(x_ref[...], gpre_ref[...]) * (1.0 + scale_ref[...]) + shift_ref[...]
        h_scr[...] = h.astype(BF16)
        acc_scr[...] = jnp.zeros_like(acc_scr)

    h = h_scr[...]
    a = _dot(h, wg_ref[...])
    u = _dot(h, wu_ref[...])
    acc_scr[...] += _dot((_silu(a) * u).astype(BF16), wd_ref[...])

    @pl.when(k == pl.num_programs(1) - 1)
    def _():
        f = _rms(acc_scr[...], gpost_ref[...])
        o_ref[...] = x_ref[...] + FFN_RES * gate_ref[...] * f


def _ffn(x, shift, scale, gate, g_pre, g_post, wg, wu, wd, rows_per_seq):
    n, d = x.shape
    dff = wg.shape[1]
    tm = _tile(rows_per_seq if shift.ndim == 3 else n, 512)
    tf = _tile(dff, 1408)
    mspec = lambda m: _mod_spec(m, tm, rows_per_seq, 2)
    return pl.pallas_call(
        _ffn_kernel,
        grid=(n // tm, dff // tf),
        in_specs=[pl.BlockSpec((tm, d), lambda i, k: (i, 0)),
                  mspec(shift), mspec(scale), mspec(gate),
                  _row_spec(d, 2), _row_spec(d, 2),
                  pl.BlockSpec((d, tf), lambda i, k: (0, k)),
                  pl.BlockSpec((d, tf), lambda i, k: (0, k)),
                  pl.BlockSpec((tf, d), lambda i, k: (k, 0))],
        out_specs=pl.BlockSpec((tm, d), lambda i, k: (i, 0)),
        out_shape=jax.ShapeDtypeStruct((n, d), F32),
        scratch_shapes=[pltpu.VMEM((tm, d), BF16), pltpu.VMEM((tm, d), F32)],
        compiler_params=_params("parallel", "arbitrary"),
    )(x, shift, scale, gate, g_pre.reshape(1, d), g_post.reshape(1, d), wg, wu, wd)


def _proj_kernel(bounds, x_ref, shift_ref, scale_ref, gpre_ref, w_ref, small_ref, xbc_ref, zg_ref, h_scr):
    j = pl.program_id(1)

    @pl.when(j == 0)
    def _():
        h = _rms(x_ref[...], gpre_ref[...]) * (1.0 + scale_ref[...]) + shift_ref[...]
        h_scr[...] = h.astype(BF16)

    res = _dot(h_scr[...], w_ref[...])
    for o_ref, lo, hi in zip((small_ref, xbc_ref, zg_ref), bounds[:-1], bounds[1:]):
        @pl.when((j >= lo) & (j < hi))
        def _():
            o_ref[...] = res.astype(o_ref.dtype)


def _proj(x, shift, scale, g_pre, w, widths, rows_per_seq):
    n, d = x.shape
    tm = _tile(rows_per_seq if shift.ndim == 3 else n, 1024)
    tn = 1024
    assert all(wd % tn == 0 for wd in widths), widths
    bounds = tuple(int(v) for v in np.cumsum((0,) + tuple(wd // tn for wd in widths)))
    mspec = lambda m: _mod_spec(m, tm, rows_per_seq, 2)

    def out_spec(lo, hi):
        return pl.BlockSpec((tm, tn), lambda i, j: (i, jnp.clip(j - lo, 0, hi - lo - 1)))

    return pl.pallas_call(
        functools.partial(_proj_kernel, bounds),
        grid=(n // tm, bounds[-1]),
        in_specs=[pl.BlockSpec((tm, d), lambda i, j: (i, 0)),
                  mspec(shift), mspec(scale), _row_spec(d, 2),
                  pl.BlockSpec((d, tn), lambda i, j: (0, j))],
        out_specs=[out_spec(lo, hi) for lo, hi in zip(bounds[:-1], bounds[1:])],
        out_shape=[jax.ShapeDtypeStruct((n, wd), dt) for wd, dt in zip(widths, (F32, BF16, BF16))],
        scratch_shapes=[pltpu.VMEM((tm, d), BF16)],
        compiler_params=_params("parallel", "arbitrary"),
    )(x, shift, scale, g_pre.reshape(1, d), w)


def _rope(x, cos, sin):
    w = x.shape[1]
    lane = lax.broadcasted_iota(jnp.int32, x.shape, 1) % LANES
    first_half = (lane >= QK_NOPE) & (lane < QK_NOPE + QK_ROPE // 2)
    swapped = jnp.where(first_half, pltpu.roll(x, w - QK_ROPE // 2, 1), pltpu.roll(x, QK_ROPE // 2, 1))
    return x * cos + swapped * sin


def _mla_kernel(prompt, small_ref, cos_ref, sin_ref, gq_ref, gkv_ref, wuq_ref, *rest):
    if prompt:
        wk_ref, wv_ref, q_ref, ckv_ref, kpe_ref, k_ref, v_ref = rest
    else:
        q_ref, ckv_ref, kpe_ref = rest
    sm = small_ref[...]
    cos, sin = cos_ref[...], sin_ref[...]
    qn = _rms(sm[:, :Q_LORA], gq_ref[...])
    q = _dot(qn.astype(BF16), wuq_ref[...])
    q = _rope(q, jnp.tile(cos, (1, MLA_HEADS)), jnp.tile(sin, (1, MLA_HEADS)))
    q_ref[...] = (q * Q_LOG2_SCALE).astype(BF16)
    ckv = _rms(sm[:, Q_LORA:Q_LORA + KV_LORA], gkv_ref[...])
    ckv_ref[...] = ckv
    kr = _rope(sm[:, KROPE_CHUNK * LANES:(KROPE_CHUNK + 1) * LANES], cos, sin)
    kpe_ref[...] = kr[:, QK_NOPE:QK_NOPE + QK_ROPE]
    if prompt:
        cb = ckv.astype(BF16)
        k_ref[...] = (_dot(cb, wk_ref[...]) + jnp.tile(kr, (1, MLA_HEADS))).astype(BF16)
        v_ref[...] = _dot(cb, wv_ref[...]).astype(BF16)


def _mla(small, cos, sin, g_q, g_kv, wuq, wk, wv, rows_per_seq, prompt):
    n = small.shape[0]
    hw = MLA_HEADS * HEAD_PAD
    tm = _tile(rows_per_seq if prompt else n, 512)
    tiles = rows_per_seq // tm if prompt else 1
    tab = pl.BlockSpec((tm, LANES), (lambda i: (i % tiles, 0)) if prompt else (lambda i: (0, 0)))
    const = lambda r, c: pl.BlockSpec((r, c), lambda i: (0, 0))
    rows = lambda c: pl.BlockSpec((tm, c), lambda i: (i, 0))
    in_specs = [rows(SMALL_W), tab, tab, const(1, Q_LORA), const(1, KV_LORA), const(Q_LORA, hw)]
    args = [small, cos, sin, g_q.reshape(1, -1), g_kv.reshape(1, -1), wuq]
    out_specs = [rows(hw), rows(KV_LORA), rows(QK_ROPE)]
    out_shape = [jax.ShapeDtypeStruct((n, hw), BF16), jax.ShapeDtypeStruct((n, KV_LORA), F32),
                 jax.ShapeDtypeStruct((n, QK_ROPE), F32)]
    if prompt:
        in_specs += [const(KV_LORA, hw), const(KV_LORA, hw)]
        args += [wk, wv]
        out_specs += [rows(hw), rows(hw)]
        out_shape += [jax.ShapeDtypeStruct((n, hw), BF16)] * 2
    return pl.pallas_call(
        functools.partial(_mla_kernel, prompt),
        grid=(n // tm,),
        in_specs=in_specs, out_specs=out_specs, out_shape=out_shape,
        compiler_params=_params("parallel"),
    )(*args)


def _attn_kernel(tq, q_ref, k_ref, v_ref, o_ref):
    j = pl.program_id(2)
    causal = (lax.broadcasted_iota(jnp.int32, (tq, tq), 1) <= lax.broadcasted_iota(jnp.int32, (tq, tq), 0))

    def one_tile(parity, head):
        lanes = slice(head * HEAD_PAD, (head + 1) * HEAD_PAD)
        q = q_ref[parity * tq:(parity + 1) * tq, lanes]

        def scores(kb):
            k = k_ref[pl.ds(pl.multiple_of(kb * tq, tq), tq), lanes]
            return lax.dot_general(q, k, _NT, preferred_element_type=F32)

        def fold(state, s, kb):
            m, l, acc = state
            v = v_ref[pl.ds(pl.multiple_of(kb * tq, tq), tq), lanes]
            m_new = jnp.maximum(m, jnp.max(s, axis=-1, keepdims=True))
            alpha = jnp.exp2(m - m_new)
            p = jnp.exp2(s - m_new)
            l = alpha * l + jnp.sum(p, axis=-1, keepdims=True)
            return m_new, l, alpha * acc + _dot(p.astype(BF16), v)

        def pair(i, state):
            s0, s1 = scores(2 * i), scores(2 * i + 1)
            return fold(fold(state, s0, 2 * i), s1, 2 * i + 1)

        init = (jnp.full((tq, 1), -jnp.inf, F32), jnp.zeros((tq, 1), F32), jnp.zeros((tq, HEAD_PAD), F32))
        state = lax.fori_loop(0, j, pair, init)
        diag = 2 * j + parity
        if parity:
            s_full, s_diag = scores(2 * j), scores(diag)
            state = fold(state, s_full, 2 * j)
        else:
            s_diag = scores(diag)
        _, l, acc = fold(state, jnp.where(causal, s_diag, -jnp.inf), diag)
        return acc / l

    for parity in range(q_ref.shape[0] // tq):
        o_first, o_second = one_tile(parity, 0), one_tile(parity, 1)
        packed = o_first + pltpu.roll(o_second, V_HEAD, 1)
        o_ref[parity * tq:(parity + 1) * tq, :] = packed.astype(BF16)


def _prompt_attention(q, k, v, bsz, t):
    assert t % 2 == 0 and MLA_HEADS % 2 == 0, t
    tq = _tile(t // 2, 512)
    nq = t // (2 * tq)
    pair_w = 2 * HEAD_PAD
    return pl.pallas_call(
        functools.partial(_attn_kernel, tq),
        grid=(bsz, MLA_HEADS // 2, nq),
        in_specs=[pl.BlockSpec((2 * tq, pair_w), lambda b, h, i: (b * nq + i, h)),
                  pl.BlockSpec((t, pair_w), lambda b, h, i: (b, h)),
                  pl.BlockSpec((t, pair_w), lambda b, h, i: (b, h))],
        out_specs=pl.BlockSpec((2 * tq, 2 * V_HEAD), lambda b, h, i: (b * nq + i, h)),
        out_shape=jax.ShapeDtypeStruct((q.shape[0], MLA_HEADS * V_HEAD), BF16),
        compiler_params=_params("parallel", "parallel", "arbitrary"),
    )(q, k, v)


def _head_mm_kernel(x_ref, w_ref, o_ref):
    o_ref[...] = _dot(x_ref[...], w_ref[...]).astype(o_ref.dtype)


def _head_mm(x, w):
    n = x.shape[0]
    nh, kin, kout = w.shape
    return pl.pallas_call(
        _head_mm_kernel,
        grid=(nh,),
        in_specs=[pl.BlockSpec((n, kin), lambda h: (0, h)),
                  pl.BlockSpec((None, kin, kout), lambda h: (h, 0, 0))],
        out_specs=pl.BlockSpec((n, kout), lambda h: (0, h)),
        out_shape=jax.ShapeDtypeStruct((n, nh * kout), BF16),
        compiler_params=_params("parallel"),
    )(x, w)


def _paged_kernel(pp, sub, per_seq, n_chunks, t_new, pt_ref, q_ref, cnew_ref, penew_ref, lat_hbm, pe_hbm,
                  o_ref, lat_buf, pe_buf, sems):
    b = pl.program_id(0)
    rows = q_ref.shape[0]

    def copies(chunk, slot, i):
        page = pt_ref[chunk * pp + i]
        return (pltpu.make_async_copy(lat_hbm.at[page], lat_buf.at[slot, i], sems.at[0, slot]),
                pltpu.make_async_copy(pe_hbm.at[page], pe_buf.at[slot, i], sems.at[1, slot]))

    def fetch(chunk, slot):
        for i in range(pp):
            lat_cp, pe_cp = copies(chunk, slot, i)
            lat_cp.start(priority=i % 2)
            pe_cp.start(priority=(i + 1) % 2)

    def wait(chunk, slot):
        for i in range(pp):
            for cp in copies(chunk, slot, i):
                cp.wait()

    @pl.when(b == 0)
    def _():
        for ahead in range(PAGED_AHEAD):
            fetch(ahead, ahead)

    q = q_ref[...]
    q_abs = q[:, :KV_LORA]
    q_pe = q[:, KV_LORA:KV_LORA + QK_ROPE]

    def fold(state, s, c):
        m_old, l_old, acc = state
        m_new = jnp.maximum(m_old, jnp.max(s, axis=-1, keepdims=True))
        alpha = jnp.exp2(m_old - m_new)
        p = jnp.exp2(s - m_new)
        l_new = alpha * l_old + jnp.sum(p, axis=-1, keepdims=True)
        return m_new, l_new, alpha * acc + _dot(p.astype(BF16), c)

    def scores(slot, g0):
        c = jnp.concatenate([lat_buf[slot, i].astype(BF16) for i in range(g0, g0 + sub)], axis=0)
        pe_t = jnp.concatenate([pe_buf[slot, i].astype(BF16) for i in range(g0, g0 + sub)], axis=1)
        return lax.dot_general(q_abs, c, _NT, preferred_element_type=F32) + _dot(q_pe, pe_t), c

    state = (jnp.full((rows, 1), -jnp.inf, F32), jnp.zeros((rows, 1), F32), jnp.zeros((rows, KV_LORA), F32))
    pending = None
    for j in range(per_seq):
        chunk = b * per_seq + j
        slot = j % PAGED_SLOTS

        @pl.when(chunk + PAGED_AHEAD < n_chunks)
        def _():
            fetch(chunk + PAGED_AHEAD, (slot + PAGED_AHEAD) % PAGED_SLOTS)

        wait(chunk, slot)
        for g0 in range(0, pp, sub):
            upcoming = scores(slot, g0)
            if pending is not None:
                state = fold(state, *pending)
            pending = upcoming
    state = fold(state, *pending)

    pad = LANES - t_new
    c = jnp.concatenate([cnew_ref[...], jnp.zeros((pad, KV_LORA), F32)], axis=0).astype(BF16)
    pe = jnp.concatenate([penew_ref[...], jnp.zeros((pad, QK_ROPE), F32)], axis=0).astype(BF16)
    s_new = (lax.dot_general(q_abs, c, _NT, preferred_element_type=F32)
             + lax.dot_general(q_pe, pe, _NT, preferred_element_type=F32))
    t_row = lax.broadcasted_iota(jnp.int32, (rows, LANES), 0) // MLA_HEADS
    key = lax.broadcasted_iota(jnp.int32, (rows, LANES), 1)
    _, l, acc = fold(state, jnp.where(key <= t_row, s_new, -jnp.inf), c)
    o_ref[...] = (acc / l).astype(BF16)


def _paged_attention(q, c_new, pe_new, cache_c, cache_pe_t, page_table):
    bsz, rows, qw = q.shape
    t_new = c_new.shape[1]
    n_pages = page_table.shape[1]
    page = cache_c.shape[1]
    assert n_pages % PAGED_SLOTS == 0, n_pages
    pp = _tile(n_pages // PAGED_SLOTS, 16)
    sub = _tile(pp, 4)
    per_seq = n_pages // pp
    seq = lambda w0, w1: pl.BlockSpec((None, w0, w1), lambda b, pt: (b, 0, 0))
    grid_spec = pltpu.PrefetchScalarGridSpec(
        num_scalar_prefetch=1,
        grid=(bsz,),
        in_specs=[seq(rows, qw), seq(t_new, KV_LORA), seq(t_new, QK_ROPE),
                  pl.BlockSpec(memory_space=pl.ANY), pl.BlockSpec(memory_space=pl.ANY)],
        out_specs=seq(rows, KV_LORA),
        scratch_shapes=[pltpu.VMEM((PAGED_SLOTS, pp, page, KV_LORA), F32),
                        pltpu.VMEM((PAGED_SLOTS, pp, QK_ROPE, page), F32),
                        pltpu.SemaphoreType.DMA((2, PAGED_SLOTS))])
    return pl.pallas_call(
        functools.partial(_paged_kernel, pp, sub, per_seq, bsz * per_seq, t_new),
        grid_spec=grid_spec,
        out_shape=jax.ShapeDtypeStruct((bsz, rows, KV_LORA), BF16),
        compiler_params=_params("arbitrary"),
    )(page_table.reshape(-1), q, c_new, pe_new, cache_c, cache_pe_t)


def _ssd_kernel(lb, nb, xs_ref, b_ref, c_ref, dt_ref, wxs_ref, wb_ref, wc_ref, bxs_ref, bb_ref, bc_ref,
                dtb_ref, alog_ref, dsk_ref, txs_ref, tb_ref, tc_ref, h0_ref,
                y_ref, hout_ref, txs_scr, tb_scr, tc_scr, h_scr):
    ci = pl.program_id(2)

    @pl.when(ci == 0)
    def _():
        txs_scr[...] = txs_ref[...]
        tb_scr[...] = tb_ref[...]
        tc_scr[...] = tc_ref[...]
        h_scr[...] = h0_ref[...].reshape(h_scr.shape)

    for s in range(nb):
        _ssd_chunk(lb, s, xs_ref, b_ref, c_ref, dt_ref, wxs_ref, wb_ref, wc_ref, bxs_ref, bb_ref, bc_ref,
                   dtb_ref, alog_ref, dsk_ref, y_ref, txs_scr, tb_scr, tc_scr, h_scr)

    @pl.when(ci == pl.num_programs(2) - 1)
    def _():
        hout_ref[...] = h_scr[...].reshape(hout_ref.shape)


def _ssd_chunk(lb, s, xs_ref, b_ref, c_ref, dt_ref, wxs_ref, wb_ref, wc_ref, bxs_ref, bb_ref, bc_ref,
               dtb_ref, alog_ref, dsk_ref, y_ref, txs_scr, tb_scr, tc_scr, h_scr):
    L = max(lb, 2 * SUBLANES)
    g = pl.program_id(1)

    def pad_rows(x):
        if lb == L:
            return x
        return jnp.concatenate([x, jnp.zeros((L - lb, x.shape[1]), x.dtype)], axis=0)

    def conv_silu(x_ref, tail_scr, w_ref, bias_ref):
        x = pad_rows(x_ref[s].astype(F32))
        tail = tail_scr[s]
        w = w_ref[...]
        row = lax.broadcasted_iota(jnp.int32, (SUBLANES, x.shape[1]), 0)
        out = bias_ref[...] + w[CONV_WIDTH - 1:CONV_WIDTH] * x
        for k in range(1, CONV_WIDTH):
            sh = pltpu.roll(x, k, 0)
            head = jnp.where(row < k, pltpu.roll(tail, k, 0), sh[:SUBLANES])
            sh = jnp.concatenate([head, sh[SUBLANES:]], axis=0)
            out = out + w[CONV_WIDTH - 1 - k:CONV_WIDTH - k] * sh
        tail_scr[s] = x[lb - SUBLANES:lb]
        return _silu(out)

    xs = conv_silu(xs_ref, txs_scr, wxs_ref, bxs_ref)
    bm = conv_silu(b_ref, tb_scr, wb_ref, bb_ref)
    cm = conv_silu(c_ref, tc_scr, wc_ref, bc_ref)

    shift = (LANES - g * HEADS_PER_GROUP) % LANES
    dt_raw = pltpu.roll(pad_rows(dt_ref[s]) + dtb_ref[...], shift, 1)
    a = -jnp.exp(pltpu.roll(alog_ref[...], shift, 1))
    dt = jax.nn.softplus(dt_raw)
    if lb < L:
        dt = jnp.where(lax.broadcasted_iota(jnp.int32, dt.shape, 0) < lb, dt, 0.0)
    la = dt * a

    r_i = lax.broadcasted_iota(jnp.int32, (L, L), 0)
    c_i = lax.broadcasted_iota(jnp.int32, (L, L), 1)
    tril = r_i >= c_i
    cum = _split3_dot(tril.astype(BF16), la)
    cum_t = cum.T
    ecum = jnp.exp(cum)
    to_end = jnp.exp(cum[L - 1:L] - cum)

    e_r = lax.broadcasted_iota(jnp.int32, (LANES, GROUP_CH), 0)
    e_c = lax.broadcasted_iota(jnp.int32, (LANES, GROUP_CH), 1) // SSM_HEAD_DIM
    expand = (e_r == e_c).astype(BF16)
    dt_e, ecum_e, to_end_e = _split2_dot_stacked([dt, ecum, to_end], expand)

    xdt = xs * dt_e
    bmb, cmb = bm.astype(BF16), cm.astype(BF16)
    cb = lax.dot_general(cmb, bmb, _NT, preferred_element_type=F32)
    half = lax.broadcasted_iota(jnp.int32, (L, LANES), 1) // SSM_HEAD_DIM
    pairs = []
    for pr in range(HEADS_PER_GROUP // 2):
        xp = xdt[:, pr * LANES:(pr + 1) * LANES]
        acc = jnp.zeros((L, LANES), F32)
        for hh in range(2):
            h = 2 * pr + hh
            seg = cum[:, h:h + 1] - cum_t[h:h + 1, :]
            w = (cb * jnp.exp(jnp.where(tril, seg, -jnp.inf))).astype(BF16)
            acc += _dot(w, jnp.where(half == hh, xp, 0.0).astype(BF16))
        pairs.append(acc)
    y_diag = jnp.concatenate(pairs, axis=1)

    h_prev = h_scr[s]
    y_off = lax.dot_general(cmb, h_prev.astype(BF16), _NT, preferred_element_type=F32) * ecum_e
    y = y_diag + y_off + dsk_ref[...] * xs
    y_ref[s] = y[:lb].astype(y_ref.dtype)

    xw_t = (xdt * to_end_e).T.astype(BF16)
    upd = _dot(xw_t, bmb)
    d_last = jnp.exp(cum_t[:, L - 1:L])
    dec = jnp.concatenate([jnp.broadcast_to(d_last[h:h + 1, :], (SSM_HEAD_DIM, SSM_STATE))
                           for h in range(HEADS_PER_GROUP)], axis=0)
    h_scr[s] = h_prev * dec + upd


def _ssd(xbc, small, conv_w, conv_b, dt_bias, a_log, d_skip, conv_tail, h0, bsz, t):
    lb = min(t, SSM_CHUNK)
    nc = t // lb
    gs, hpg = SSM_GROUPS, HEADS_PER_GROUP
    b_blk0 = SSM_INNER // SSM_STATE
    c_blk0 = b_blk0 + gs
    pad128 = lambda v: jnp.pad(v, (0, LANES - v.shape[0])).reshape(1, LANES)
    dsk = jnp.repeat(d_skip, SSM_HEAD_DIM).reshape(1, SSM_INNER)
    cbias = conv_b.reshape(1, CONV_DIM)

    nb = _tile(bsz, 8)
    seq3 = lambda w, col: pl.BlockSpec((nb, lb, w), lambda b, g, c: (b, c, col(g)))
    par2 = lambda r, w, col: pl.BlockSpec((r, w), lambda b, g, c: (0, col(g)))
    tail3 = lambda w, col: pl.BlockSpec((nb, SUBLANES, w), lambda b, g, c: (b, 0, col(g)))
    xs_col = lambda g: g
    b_col = lambda g: b_blk0 + g
    c_col = lambda g: c_blk0 + g
    zero = lambda g: 0
    state_spec = pl.BlockSpec((nb, hpg, SSM_HEAD_DIM, SSM_STATE), lambda b, g, c: (b, g, 0, 0))
    y, h_out = pl.pallas_call(
        functools.partial(_ssd_kernel, lb, nb),
        grid=(bsz // nb, gs, nc),
        in_specs=[seq3(GROUP_CH, xs_col), seq3(SSM_STATE, b_col), seq3(SSM_STATE, c_col),
                  seq3(LANES, lambda g: DT_CHUNK),
                  par2(CONV_WIDTH, GROUP_CH, xs_col), par2(CONV_WIDTH, SSM_STATE, b_col),
                  par2(CONV_WIDTH, SSM_STATE, c_col),
                  par2(1, GROUP_CH, xs_col), par2(1, SSM_STATE, b_col), par2(1, SSM_STATE, c_col),
                  par2(1, LANES, zero), par2(1, LANES, zero), par2(1, GROUP_CH, xs_col),
                  tail3(GROUP_CH, xs_col), tail3(SSM_STATE, b_col), tail3(SSM_STATE, c_col),
                  state_spec],
        out_specs=[seq3(GROUP_CH, xs_col), state_spec],
        out_shape=[jax.ShapeDtypeStruct((bsz, t, SSM_INNER), BF16),
                   jax.ShapeDtypeStruct(h0.shape, F32)],
        scratch_shapes=[pltpu.VMEM((nb, SUBLANES, GROUP_CH), F32), pltpu.VMEM((nb, SUBLANES, SSM_STATE), F32),
                        pltpu.VMEM((nb, SUBLANES, SSM_STATE), F32),
                        pltpu.VMEM((nb, GROUP_CH, SSM_STATE), F32)],
        compiler_params=_params("parallel", "parallel", "arbitrary"),
    )(xbc, xbc, xbc, small, conv_w, conv_w, conv_w, cbias, cbias, cbias,
      pad128(dt_bias), pad128(a_log), dsk, conv_tail, conv_tail, conv_tail, h0)
    return y, h_out


def _mix_kernel(x_ref, attn_ref, y_ref, z_ref, gates_ref, gate_ref, gssm_ref, gpost_ref,
                woa_ref, wos_ref, wout_ref, o_ref):
    o_attn = _dot(attn_ref[...], woa_ref[...])
    yz = y_ref[...].astype(F32) * _silu(z_ref[...].astype(F32))
    o_ssm = _dot(_rms(yz, gssm_ref[...]).astype(BF16), wos_ref[...])
    gates = gates_ref[...].astype(F32)
    mixed = (jax.nn.sigmoid(gates[:, :D_MODEL]) * o_attn
             + jax.nn.sigmoid(gates[:, D_MODEL:]) * o_ssm)
    m2 = _dot(mixed.astype(BF16), wout_ref[...])
    o_ref[...] = x_ref[...] + gate_ref[...] * _rms(m2, gpost_ref[...])


def _mix(x, attn, y, zg, gate, g_ssm, g_post, woa, wos, wout, rows_per_seq):
    n, d = x.shape
    tm = _tile(rows_per_seq if gate.ndim == 3 else n, 256)
    rows = lambda w, col: pl.BlockSpec((tm, w), lambda i: (i, col))
    const = lambda r, c: pl.BlockSpec((r, c), lambda i: (0, 0))
    return pl.pallas_call(
        _mix_kernel,
        grid=(n // tm,),
        in_specs=[rows(d, 0), rows(attn.shape[1], 0), rows(SSM_INNER, 0), rows(SSM_INNER, 0),
                  rows(2 * d, 1), _mod_spec(gate, tm, rows_per_seq, 1),
                  const(1, SSM_INNER), const(1, d),
                  const(*woa.shape), const(*wos.shape), const(*wout.shape)],
        out_specs=rows(d, 0),
        out_shape=jax.ShapeDtypeStruct((n, d), F32),
        compiler_params=_params("parallel"),
    )(x, attn, y, zg, zg, gate, g_ssm.reshape(1, -1), g_post.reshape(1, -1), woa, wos, wout)


def _prep_weights(w_in, w_uq, w_ukv, w_o_attn):
    o = IN_OFFSETS
    seg = lambda i: w_in[:, o[i]:o[i + 1]]
    zeros = lambda c: jnp.zeros((D_MODEL, c), w_in.dtype)
    w_small = jnp.concatenate([seg(0), seg(1), zeros(QK_NOPE), seg(2), zeros(LANES - QK_NOPE - QK_ROPE),
                               seg(5), zeros(LANES - SSM_HEADS)], axis=1)
    w_xbc = seg(4)
    w_zg = jnp.concatenate([seg(3), seg(6), seg(7)], axis=1)
    hd = QK_NOPE + QK_ROPE
    wuq = jnp.pad(w_uq.reshape(Q_LORA, MLA_HEADS, hd), ((0, 0), (0, 0), (0, HEAD_PAD - hd)))
    wuq = wuq.reshape(Q_LORA, MLA_HEADS * HEAD_PAD)
    wkv = w_ukv.reshape(KV_LORA, MLA_HEADS, QK_NOPE + V_HEAD)
    w_uk, w_uv = wkv[..., :QK_NOPE], wkv[..., QK_NOPE:]
    pad_h = lambda w, width: jnp.pad(w, ((0, 0), (0, 0), (0, HEAD_PAD - width)))
    wk = pad_h(w_uk, QK_NOPE).reshape(KV_LORA, MLA_HEADS * HEAD_PAD)
    wv = pad_h(w_uv, V_HEAD).reshape(KV_LORA, MLA_HEADS * HEAD_PAD)
    w_abs = jnp.zeros((MLA_HEADS, HEAD_PAD, KV_LORA + LANES), w_ukv.dtype)
    w_abs = w_abs.at[:, :QK_NOPE, :KV_LORA].set(jnp.transpose(w_uk, (1, 2, 0)))
    w_abs = w_abs.at[:, QK_NOPE:hd, KV_LORA:KV_LORA + QK_ROPE].set(jnp.eye(QK_ROPE, dtype=w_ukv.dtype))
    w_uv_h = pad_h(jnp.transpose(w_uv, (1, 0, 2)), V_HEAD)
    woa = jnp.pad(w_o_attn.reshape(MLA_HEADS, V_HEAD, D_MODEL), ((0, 0), (0, HEAD_PAD - V_HEAD), (0, 0)))
    woa = woa.reshape(MLA_HEADS * HEAD_PAD, D_MODEL)
    cast = lambda w: w.astype(BF16)
    return tuple(map(cast, (w_small, w_xbc, w_zg, wuq, wk, wv, w_abs, w_uv_h, woa)))


def _rope_tables(pos):
    half = QK_ROPE // 2
    inv = ROPE_THETA ** (-jnp.arange(half, dtype=F32) / half)
    ang = pos.astype(F32)[:, None] * inv[None, :]
    cos, sin = jnp.cos(ang), jnp.sin(ang)
    n = pos.shape[0]
    ones, zeros = jnp.ones((n, QK_NOPE), F32), jnp.zeros((n, QK_NOPE), F32)
    tail = jnp.zeros((n, LANES - QK_NOPE - QK_ROPE), F32)
    return (jnp.concatenate([ones, cos, cos, tail], axis=1),
            jnp.concatenate([zeros, -sin, sin, tail], axis=1))


def _layer(x, mods, pos, per_row, conv_prev, ssm_prev, paged, lw):
    (g_pre, g_post, wg, wu, wd, w_small, w_xbc, w_zg, g_q, wuq, g_kv, wk, wv, w_abs, w_uv_h, woa,
     conv_w, conv_b, dt_bias, a_log, d_skip, g_ssm, wos, wout) = lw
    bsz, t, d = x.shape
    n = bsz * t
    x2 = x.reshape(n, d)
    if per_row:
        mod = lambda s, k: jnp.repeat(mods[:, s, k], t, axis=0)
    else:
        mod = lambda s, k: mods[:, s, k][:, None, :]

    x2 = _ffn(x2, mod(0, 0), mod(0, 1), mod(0, 2), g_pre[0], g_post[0], wg[0], wu[0], wd[0], t)

    sh, sc = mod(1, 0), mod(1, 1)
    w_proj = jnp.concatenate([w_small, w_xbc, w_zg], axis=1)
    small, xbc, zg = _proj(x2, sh, sc, g_pre[1], w_proj,
                           (w_small.shape[1], w_xbc.shape[1], w_zg.shape[1]), t)

    cos, sin = _rope_tables(pos)
    if per_row:
        tm = _tile(n, 512)
        cos, sin = jnp.tile(cos, (tm // t, 1)), jnp.tile(sin, (tm // t, 1))
    mla = _mla(small, cos, sin, g_q, g_kv, wuq, wk, wv, t, paged is None)
    if paged is None:
        q, c_kv, k_pe, k, v = mla
        attn = _prompt_attention(q, k, v, bsz, t)
    else:
        q, c_kv, k_pe = mla
        cache_c, cache_pe, page_table = paged
        q_full = _head_mm(q, w_abs).reshape(bsz, t * MLA_HEADS, KV_LORA + LANES)
        o_lat = _paged_attention(q_full, c_kv.reshape(bsz, t, KV_LORA), k_pe.reshape(bsz, t, QK_ROPE),
                                 cache_c, cache_pe, page_table)
        attn = _head_mm(o_lat.reshape(n, MLA_HEADS * KV_LORA), w_uv_h)

    xbc3 = xbc.reshape(bsz, t, CONV_DIM)
    tail = jnp.pad(conv_prev.astype(F32), ((0, 0), (SUBLANES - (CONV_WIDTH - 1), 0), (0, 0)))
    y, new_ssm = _ssd(xbc3, small.reshape(bsz, t, SMALL_W), conv_w, conv_b, dt_bias, a_log, d_skip,
                      tail, ssm_prev, bsz, t)
    u_tail = jnp.concatenate([conv_prev.astype(F32), xbc3[:, -(CONV_WIDTH - 1):].astype(F32)], axis=1)
    new_conv = u_tail[:, -(CONV_WIDTH - 1):]

    if paged is None:
        woa = woa.reshape(MLA_HEADS, HEAD_PAD, D_MODEL)[:, :V_HEAD].reshape(MLA_HEADS * V_HEAD, D_MODEL)
    x2 = _mix(x2, attn, y.reshape(n, SSM_INNER), zg, mod(1, 2), g_ssm, g_post[1], woa, wos, wout, t)
    x2 = _ffn(x2, mod(2, 0), mod(2, 1), mod(2, 2), g_pre[2], g_post[2], wg[1], wu[1], wd[1], t)
    return (x2.reshape(bsz, t, d), c_kv.reshape(bsz, t, KV_LORA), k_pe.reshape(bsz, t, QK_ROPE),
            new_conv, new_ssm)


def kernel(x_prompt, x_sample, cache_kv_latent, cache_k_rope, state_conv, state_ssm, page_table,
           c_prompt, c_sample, w_ada, b_ada, g_pre, g_post, w_ffn_gate, w_ffn_up, w_ffn_down, w_in,
           g_q_lat, w_uq, g_kv_lat, w_ukv, w_o_attn, conv_w, conv_b, dt_bias, a_log, d_skip,
           g_ssm_norm, w_o_ssm, w_out):
    bp, t_prompt = x_prompt.shape[:2]
    bs, t_sample = x_sample.shape[:2]
    depth = w_in.shape[0]
    past_len = page_table.shape[1] * cache_kv_latent.shape[2]
    pos_p = jnp.arange(t_prompt, dtype=jnp.int32)
    pos_s = past_len + jnp.arange(t_sample, dtype=jnp.int32)
    yp, ys = x_prompt, x_sample
    outs = [[] for _ in range(8)]
    pad_c = (-(bp + bs)) % SUBLANES
    c_all = jnp.concatenate([c_prompt, c_sample, jnp.zeros((pad_c, D_MODEL), c_prompt.dtype)], axis=0)
    for l in range(depth):
        mods = _ada(c_all, w_ada[l], b_ada[l]).reshape(-1, N_SUB, 3, D_MODEL)
        prepped = _prep_weights(w_in[l], w_uq[l], w_ukv[l], w_o_attn[l])
        w_small, w_xbc, w_zg, wuq, wk, wv, w_abs, w_uv_h, woa = prepped
        cast = lambda w: w.astype(BF16)
        lw = (g_pre[l], g_post[l], cast(w_ffn_gate[l]), cast(w_ffn_up[l]), cast(w_ffn_down[l]),
              w_small, w_xbc, w_zg, g_q_lat[l], wuq, g_kv_lat[l], wk, wv, w_abs, w_uv_h, woa,
              conv_w[l], conv_b[l], dt_bias[l], a_log[l], d_skip[l], g_ssm_norm[l],
              cast(w_o_ssm[l]), cast(w_out[l]))
        conv0 = jnp.zeros((bp, CONV_WIDTH - 1, CONV_DIM), x_prompt.dtype)
        ssm0 = jnp.zeros((bp, SSM_HEADS, SSM_HEAD_DIM, SSM_STATE), state_ssm.dtype)
        res_p = _layer(yp, mods[:bp], pos_p, False, conv0, ssm0, None, lw)
        paged = (cache_kv_latent[l], jnp.swapaxes(cache_k_rope[l], 1, 2), page_table)
        res_s = _layer(ys, mods[bp:bp + bs], pos_s, True, state_conv[l], state_ssm[l], paged, lw)
        yp, ys = res_p[0], res_s[0]
        for i in range(4):
            outs[i].append(res_p[1 + i])
            outs[4 + i].append(res_s[1 + i])
    return (yp, ys) + tuple(jnp.stack(o) for o in outs)
```

```python
import functools

import jax
import jax.numpy as jnp
import numpy as np
from jax import lax
from jax.experimental import pallas as pl
from jax.experimental.pallas import tpu as pltpu

D_MODEL = 1024
MLA_HEADS = 16
QK_NOPE = 64
QK_ROPE = 32
V_HEAD = 64
Q_LORA = 512
KV_LORA = 256
ROPE_THETA = 10000.0
ATTN_SCALE = (QK_NOPE + QK_ROPE) ** -0.5
Q_LOG2_SCALE = ATTN_SCALE * float(np.log2(np.e))
SSM_INNER = 2 * D_MODEL
SSM_HEAD_DIM = 64
SSM_HEADS = SSM_INNER // SSM_HEAD_DIM
SSM_GROUPS = 4
SSM_STATE = 128
CONV_WIDTH = 4
SSM_CHUNK = 128
CONV_DIM = SSM_INNER + 2 * SSM_GROUPS * SSM_STATE
D_FF = 2816
FFN_RES = 0.5
N_SUB = 3
EPS = 1e-6
IN_SIZES = (Q_LORA, KV_LORA, QK_ROPE, SSM_INNER, CONV_DIM, SSM_HEADS, D_MODEL, D_MODEL)
IN_OFFSETS = tuple(int(v) for v in np.cumsum((0,) + IN_SIZES))

LANES = 128
SUBLANES = 8
VMEM_LIMIT_BYTES = 56 * 1024 * 1024

PAGED_SLOTS = 4
PAGED_AHEAD = 2
HEAD_PAD = LANES
HEADS_PER_GROUP = SSM_HEADS // SSM_GROUPS
GROUP_CH = HEADS_PER_GROUP * SSM_HEAD_DIM
SMALL_W = Q_LORA + KV_LORA + 2 * LANES
KROPE_CHUNK = (Q_LORA + KV_LORA) // LANES
DT_CHUNK = KROPE_CHUNK + 1

BF16 = jnp.bfloat16
F32 = jnp.float32
_NT = (((1,), (1,)), ((), ()))


def _tile(n, target):
    t = min(n, target)
    while n % t:
        t -= 1
    return t


def _params(*sem):
    return pltpu.CompilerParams(dimension_semantics=sem, vmem_limit_bytes=VMEM_LIMIT_BYTES)


def _rms(x, g):
    return x * lax.rsqrt(jnp.mean(x * x, axis=-1, keepdims=True) + EPS) * g


def _silu(x):
    return x * jax.nn.sigmoid(x)


def _dot(a, b):
    return jnp.dot(a, b, preferred_element_type=F32)


def _split2_dot(v, m):
    hi = v.astype(BF16)
    mid = (v - hi.astype(F32)).astype(BF16)
    return _dot(hi, m) + _dot(mid, m)


def _split2_dot_stacked(vs, m):
    rows = vs[0].shape[0]
    parts = []
    for v in vs:
        hi = v.astype(BF16)
        parts += [hi, (v - hi.astype(F32)).astype(BF16)]
    out = _dot(jnp.concatenate(parts, axis=0), m)
    return [out[2 * i * rows:(2 * i + 1) * rows] + out[(2 * i + 1) * rows:(2 * i + 2) * rows]
            for i in range(len(vs))]


def _split3_dot(m, v):
    hi = v.astype(BF16)
    r1 = v - hi.astype(F32)
    mid = r1.astype(BF16)
    lo = (r1 - mid.astype(F32)).astype(BF16)
    return _dot(m, hi) + _dot(m, mid) + _dot(m, lo)


def _mod_spec(mod, tm, rows_per_seq, ngrid):
    d = mod.shape[-1]
    if mod.ndim == 3:
        tiles = rows_per_seq // tm
        if ngrid == 1:
            return pl.BlockSpec((None, 1, d), lambda i: (i // tiles, 0, 0))
        return pl.BlockSpec((None, 1, d), lambda i, j: (i // tiles, 0, 0))
    if ngrid == 1:
        return pl.BlockSpec((tm, d), lambda i: (i, 0))
    return pl.BlockSpec((tm, d), lambda i, j: (i, 0))


def _row_spec(width, ngrid):
    if ngrid == 1:
        return pl.BlockSpec((1, width), lambda i: (0, 0))
    return pl.BlockSpec((1, width), lambda i, j: (0, 0))


def _ada_kernel(c_ref, w_ref, b_ref, o_ref):
    h = _silu(c_ref[...]).astype(BF16)
    o_ref[...] = _dot(h, w_ref[...].astype(BF16)) + b_ref[...]


def _ada(c, w, b):
    m, d = c.shape
    n = w.shape[1]
    tn = _tile(n, 1024)
    return pl.pallas_call(
        _ada_kernel,
        grid=(n // tn,),
        in_specs=[pl.BlockSpec((m, d), lambda j: (0, 0)),
                  pl.BlockSpec((d, tn), lambda j: (0, j)),
                  pl.BlockSpec((1, tn), lambda j: (0, j))],
        out_specs=pl.BlockSpec((m, tn), lambda j: (0, j)),
        out_shape=jax.ShapeDtypeStruct((m, n), F32),
        compiler_params=_params("arbitrary"),
    )(c, w, b.reshape(1, n))


def _ffn_kernel(x_ref, shift_ref, scale_ref, gate_ref, gpre_ref, gpost_ref, wg_ref, wu_ref, wd_ref,
                o_ref, h_scr, acc_scr):
    k = pl.program_id(1)

    @pl.when(k == 0)
    def _():
        h = _rms(x_ref[...], gpre_ref[...]) * (1.0 + scale_ref[...]) + shift_ref[...]
        h_scr[...] = h.astype(BF16)
        acc_scr[...] = jnp.zeros_like(acc_scr)

    h = h_scr[...]
    a = _dot(h, wg_ref[...])
    u = _dot(h, wu_ref[...])
    acc_scr[...] += _dot((_silu(a) * u).astype(BF16), wd_ref[...])

    @pl.when(k == pl.num_programs(1) - 1)
    def _():
        f = _rms(acc_scr[...], gpost_ref[...])
        o_ref[...] = x_ref[...] + FFN_RES * gate_ref[...] * f


def _ffn(x, shift, scale, gate, g_pre, g_post, wg, wu, wd, rows_per_seq):
    n, d = x.shape
    dff = wg.shape[1]
    tm = _tile(rows_per_seq if shift.ndim == 3 else n, 512)
    tf = _tile(dff, 1408)
    mspec = lambda m: _mod_spec(m, tm, rows_per_seq, 2)
    return pl.pallas_call(
        _ffn_kernel,
        grid=(n // tm, dff // tf),
        in_specs=[pl.BlockSpec((tm, d), lambda i, k: (i, 0)),
                  mspec(shift), mspec(scale), mspec(gate),
                  _row_spec(d, 2), _row_spec(d, 2),
                  pl.BlockSpec((d, tf), lambda i, k: (0, k)),
                  pl.BlockSpec((d, tf), lambda i, k: (0, k)),
                  pl.BlockSpec((tf, d), lambda i, k: (k, 0))],
        out_specs=pl.BlockSpec((tm, d), lambda i, k: (i, 0)),
        out_shape=jax.ShapeDtypeStruct((n, d), F32),
        scratch_shapes=[pltpu.VMEM((tm, d), BF16), pltpu.VMEM((tm, d), F32)],
        compiler_params=_params("parallel", "arbitrary"),
    )(x, shift, scale, gate, g_pre.reshape(1, d), g_post.reshape(1, d), wg, wu, wd)


def _proj_kernel(bounds, x_ref, shift_ref, scale_ref, gpre_ref, w_ref, small_ref, xbc_ref, zg_ref, h_scr):
    j = pl.program_id(1)

    @pl.when(j == 0)
    def _():
        h = _rms(x_ref[...], gpre_ref[...]) * (1.0 + scale_ref[...]) + shift_ref[...]
        h_scr[...] = h.astype(BF16)

    res = _dot(h_scr[...], w_ref[...])
    for o_ref, lo, hi in zip((small_ref, xbc_ref, zg_ref), bounds[:-1], bounds[1:]):
        @pl.when((j >= lo) & (j < hi))
        def _():
            o_ref[...] = res.astype(o_ref.dtype)


def _proj(x, shift, scale, g_pre, w, widths, rows_per_seq):
    n, d = x.shape
    tm = _tile(rows_per_seq if shift.ndim == 3 else n, 1024)
    tn = 1024
    assert all(wd % tn == 0 for wd in widths), widths
    bounds = tuple(int(v) for v in np.cumsum((0,) + tuple(wd // tn for wd in widths)))
    mspec = lambda m: _mod_spec(m, tm, rows_per_seq, 2)

    def out_spec(lo, hi):
        return pl.BlockSpec((tm, tn), lambda i, j: (i, jnp.clip(j - lo, 0, hi - lo - 1)))

    return pl.pallas_call(
        functools.partial(_proj_kernel, bounds),
        grid=(n // tm, bounds[-1]),
        in_specs=[pl.BlockSpec((tm, d), lambda i, j: (i, 0)),
                  mspec(shift), mspec(scale), _row_spec(d, 2),
                  pl.BlockSpec((d, tn), lambda i, j: (0, j))],
        out_specs=[out_spec(lo, hi) for lo, hi in zip(bounds[:-1], bounds[1:])],
        out_shape=[jax.ShapeDtypeStruct((n, wd), dt) for wd, dt in zip(widths, (F32, BF16, BF16))],
        scratch_shapes=[pltpu.VMEM((tm, d), BF16)],
        compiler_params=_params("parallel", "arbitrary"),
    )(x, shift, scale, g_pre.reshape(1, d), w)


def _rope(x, cos, sin):
    w = x.shape[1]
    lane = lax.broadcasted_iota(jnp.int32, x.shape, 1) % LANES
    first_half = (lane >= QK_NOPE) & (lane < QK_NOPE + QK_ROPE // 2)
    swapped = jnp.where(first_half, pltpu.roll(x, w - QK_ROPE // 2, 1), pltpu.roll(x, QK_ROPE // 2, 1))
    return x * cos + swapped * sin


def _mla_kernel(prompt, small_ref, cos_ref, sin_ref, gq_ref, gkv_ref, wuq_ref, *rest):
    if prompt:
        wk_ref, wv_ref, q_ref, ckv_ref, kpe_ref, k_ref, v_ref = rest
    else:
        q_ref, ckv_ref, kpe_ref = rest
    sm = small_ref[...]
    cos, sin = cos_ref[...], sin_ref[...]
    qn = _rms(sm[:, :Q_LORA], gq_ref[...])
    q = _dot(qn.astype(BF16), wuq_ref[...])
    q = _rope(q, jnp.tile(cos, (1, MLA_HEADS)), jnp.tile(sin, (1, MLA_HEADS)))
    q_ref[...] = (q * Q_LOG2_SCALE).astype(BF16)
    ckv = _rms(sm[:, Q_LORA:Q_LORA + KV_LORA], gkv_ref[...])
    ckv_ref[...] = ckv
    kr = _rope(sm[:, KROPE_CHUNK * LANES:(KROPE_CHUNK + 1) * LANES], cos, sin)
    kpe_ref[...] = kr[:, QK_NOPE:QK_NOPE + QK_ROPE]
    if prompt:
        cb = ckv.astype(BF16)
        k_ref[...] = (_dot(cb, wk_ref[...]) + jnp.tile(kr, (1, MLA_HEADS))).astype(BF16)
        v_ref[...] = _dot(cb, wv_ref[...]).astype(BF16)


def _mla(small, cos, sin, g_q, g_kv, wuq, wk, wv, rows_per_seq, prompt):
    n = small.shape[0]
    hw = MLA_HEADS * HEAD_PAD
    tm = _tile(rows_per_seq if prompt else n, 512)
    tiles = rows_per_seq // tm if prompt else 1
    tab = pl.BlockSpec((tm, LANES), (lambda i: (i % tiles, 0)) if prompt else (lambda i: (0, 0)))
    const = lambda r, c: pl.BlockSpec((r, c), lambda i: (0, 0))
    rows = lambda c: pl.BlockSpec((tm, c), lambda i: (i, 0))
    in_specs = [rows(SMALL_W), tab, tab, const(1, Q_LORA), const(1, KV_LORA), const(Q_LORA, hw)]
    args = [small, cos, sin, g_q.reshape(1, -1), g_kv.reshape(1, -1), wuq]
    out_specs = [rows(hw), rows(KV_LORA), rows(QK_ROPE)]
    out_shape = [jax.ShapeDtypeStruct((n, hw), BF16), jax.ShapeDtypeStruct((n, KV_LORA), F32),
                 jax.ShapeDtypeStruct((n, QK_ROPE), F32)]
    if prompt:
        in_specs += [const(KV_LORA, hw), const(KV_LORA, hw)]
        args += [wk, wv]
        out_specs += [rows(hw), rows(hw)]
        out_shape += [jax.ShapeDtypeStruct((n, hw), BF16)] * 2
    return pl.pallas_call(
        functools.partial(_mla_kernel, prompt),
        grid=(n // tm,),
        in_specs=in_specs, out_specs=out_specs, out_shape=out_shape,
        compiler_params=_params("parallel"),
    )(*args)


def _attn_kernel(tq, q_ref, k_ref, v_ref, o_ref):
    j = pl.program_id(2)
    causal = (lax.broadcasted_iota(jnp.int32, (tq, tq), 1) <= lax.broadcasted_iota(jnp.int32, (tq, tq), 0))

    def one_tile(parity, head):
        lanes = slice(head * HEAD_PAD, (head + 1) * HEAD_PAD)
        q = q_ref[parity * tq:(parity + 1) * tq, lanes]

        def scores(kb):
            k = k_ref[pl.ds(pl.multiple_of(kb * tq, tq), tq), lanes]
            return lax.dot_general(q, k, _NT, preferred_element_type=F32)

        def fold(state, s, kb):
            m, l, acc = state
            v = v_ref[pl.ds(pl.multiple_of(kb * tq, tq), tq), lanes]
            m_new = jnp.maximum(m, jnp.max(s, axis=-1, keepdims=True))
            alpha = jnp.exp2(m - m_new)
            p = jnp.exp2(s - m_new)
            l = alpha * l + jnp.sum(p, axis=-1, keepdims=True)
            return m_new, l, alpha * acc + _dot(p.astype(BF16), v)

        def pair(i, state):
            s0, s1 = scores(2 * i), scores(2 * i + 1)
            return fold(fold(state, s0, 2 * i), s1, 2 * i + 1)

        init = (jnp.full((tq, 1), -jnp.inf, F32), jnp.zeros((tq, 1), F32), jnp.zeros((tq, HEAD_PAD), F32))
        state = lax.fori_loop(0, j, pair, init)
        diag = 2 * j + parity
        if parity:
            s_full, s_diag = scores(2 * j), scores(diag)
            state = fold(state, s_full, 2 * j)
        else:
            s_diag = scores(diag)
        _, l, acc = fold(state, jnp.where(causal, s_diag, -jnp.inf), diag)
        return acc / l

    for parity in range(q_ref.shape[0] // tq):
        o_first, o_second = one_tile(parity, 0), one_tile(parity, 1)
        packed = o_first + pltpu.roll(o_second, V_HEAD, 1)
        o_ref[parity * tq:(parity + 1) * tq, :] = packed.astype(BF16)


def _prompt_attention(q, k, v, bsz, t):
    assert t % 2 == 0 and MLA_HEADS % 2 == 0, t
    tq = _tile(t // 2, 512)
    nq = t // (2 * tq)
    pair_w = 2 * HEAD_PAD
    return pl.pallas_call(
        functools.partial(_attn_kernel, tq),
        grid=(bsz, MLA_HEADS // 2, nq),
        in_specs=[pl.BlockSpec((2 * tq, pair_w), lambda b, h, i: (b * nq + i, h)),
                  pl.BlockSpec((t, pair_w), lambda b, h, i: (b, h)),
                  pl.BlockSpec((t, pair_w), lambda b, h, i: (b, h))],
        out_specs=pl.BlockSpec((2 * tq, 2 * V_HEAD), lambda b, h, i: (b * nq + i, h)),
        out_shape=jax.ShapeDtypeStruct((q.shape[0], MLA_HEADS * V_HEAD), BF16),
        compiler_params=_params("parallel", "parallel", "arbitrary"),
    )(q, k, v)


def _head_mm_kernel(x_ref, w_ref, o_ref):
    o_ref[...] = _dot(x_ref[...], w_ref[...]).astype(o_ref.dtype)


def _head_mm(x, w):
    n = x.shape[0]
    nh, kin, kout = w.shape
    return pl.pallas_call(
        _head_mm_kernel,
        grid=(nh,),
        in_specs=[pl.BlockSpec((n, kin), lambda h: (0, h)),
                  pl.BlockSpec((None, kin, kout), lambda h: (h, 0, 0))],
        out_specs=pl.BlockSpec((n, kout), lambda h: (0, h)),
        out_shape=jax.ShapeDtypeStruct((n, nh * kout), BF16),
        compiler_params=_params("parallel"),
    )(x, w)


def _paged_kernel(pp, sub, per_seq, n_chunks, t_new, pt_ref, q_ref, cnew_ref, penew_ref, lat_hbm, pe_hbm,
                  o_ref, lat_buf, pe_buf, sems):
    b = pl.program_id(0)
    rows = q_ref.shape[0]

    def copies(chunk, slot, i):
        page = pt_ref[chunk * pp + i]
        return (pltpu.make_async_copy(lat_hbm.at[page], lat_buf.at[slot, i], sems.at[0, slot]),
                pltpu.make_async_copy(pe_hbm.at[page], pe_buf.at[slot, i], sems.at[1, slot]))

    def fetch(chunk, slot):
        for i in range(pp):
            lat_cp, pe_cp = copies(chunk, slot, i)
            lat_cp.start(priority=i % 2)
            pe_cp.start(priority=(i + 1) % 2)

    def wait(chunk, slot):
        for i in range(pp):
            for cp in copies(chunk, slot, i):
                cp.wait()

    @pl.when(b == 0)
    def _():
        for ahead in range(PAGED_AHEAD):
            fetch(ahead, ahead)

    q = q_ref[...]
    q_abs = q[:, :KV_LORA]
    q_pe = q[:, KV_LORA:KV_LORA + QK_ROPE]

    def fold(state, s, c):
        m_old, l_old, acc = state
        m_new = jnp.maximum(m_old, jnp.max(s, axis=-1, keepdims=True))
        alpha = jnp.exp2(m_old - m_new)
        p = jnp.exp2(s - m_new)
        l_new = alpha * l_old + jnp.sum(p, axis=-1, keepdims=True)
        return m_new, l_new, alpha * acc + _dot(p.astype(BF16), c)

    def scores(slot, g0):
        c = jnp.concatenate([lat_buf[slot, i].astype(BF16) for i in range(g0, g0 + sub)], axis=0)
        pe_t = jnp.concatenate([pe_buf[slot, i].astype(BF16) for i in range(g0, g0 + sub)], axis=1)
        return lax.dot_general(q_abs, c, _NT, preferred_element_type=F32) + _dot(q_pe, pe_t), c

    state = (jnp.full((rows, 1), -jnp.inf, F32), jnp.zeros((rows, 1), F32), jnp.zeros((rows, KV_LORA), F32))
    pending = None
    for j in range(per_seq):
        chunk = b * per_seq + j
        slot = j % PAGED_SLOTS

        @pl.when(chunk + PAGED_AHEAD < n_chunks)
        def _():
            fetch(chunk + PAGED_AHEAD, (slot + PAGED_AHEAD) % PAGED_SLOTS)

        wait(chunk, slot)
        for g0 in range(0, pp, sub):
            upcoming = scores(slot, g0)
            if pending is not None:
                state = fold(state, *pending)
            pending = upcoming
    state = fold(state, *pending)

    pad = LANES - t_new
    c = jnp.concatenate([cnew_ref[...], jnp.zeros((pad, KV_LORA), F32)], axis=0).astype(BF16)
    pe = jnp.concatenate([penew_ref[...], jnp.zeros((pad, QK_ROPE), F32)], axis=0).astype(BF16)
    s_new = (lax.dot_general(q_abs, c, _NT, preferred_element_type=F32)
             + lax.dot_general(q_pe, pe, _NT, preferred_element_type=F32))
    t_row = lax.broadcasted_iota(jnp.int32, (rows, LANES), 0) // MLA_HEADS
    key = lax.broadcasted_iota(jnp.int32, (rows, LANES), 1)
    _, l, acc = fold(state, jnp.where(key <= t_row, s_new, -jnp.inf), c)
    o_ref[...] = (acc / l).astype(BF16)


def _paged_attention(q, c_new, pe_new, cache_c, cache_pe_t, page_table):
    bsz, rows, qw = q.shape
    t_new = c_new.shape[1]
    n_pages = page_table.shape[1]
    page = cache_c.shape[1]
    assert n_pages % PAGED_SLOTS == 0, n_pages
    pp = _tile(n_pages // PAGED_SLOTS, 16)
    sub = _tile(pp, 4)
    per_seq = n_pages // pp
    seq = lambda w0, w1: pl.BlockSpec((None, w0, w1), lambda b, pt: (b, 0, 0))
    grid_spec = pltpu.PrefetchScalarGridSpec(
        num_scalar_prefetch=1,
        grid=(bsz,),
        in_specs=[seq(rows, qw), seq(t_new, KV_LORA), seq(t_new, QK_ROPE),
                  pl.BlockSpec(memory_space=pl.ANY), pl.BlockSpec(memory_space=pl.ANY)],
        out_specs=seq(rows, KV_LORA),
        scratch_shapes=[pltpu.VMEM((PAGED_SLOTS, pp, page, KV_LORA), F32),
                        pltpu.VMEM((PAGED_SLOTS, pp, QK_ROPE, page), F32),
                        pltpu.SemaphoreType.DMA((2, PAGED_SLOTS))])
    return pl.pallas_call(
        functools.partial(_paged_kernel, pp, sub, per_seq, bsz * per_seq, t_new),
        grid_spec=grid_spec,
        out_shape=jax.ShapeDtypeStruct((bsz, rows, KV_LORA), BF16),
        compiler_params=_params("arbitrary"),
    )(page_table.reshape(-1), q, c_new, pe_new, cache_c, cache_pe_t)


def _ssd_kernel(lb, nb, xs_ref, b_ref, c_ref, dt_ref, wxs_ref, wb_ref, wc_ref, bxs_ref, bb_ref, bc_ref,
                dtb_ref, alog_ref, dsk_ref, txs_ref, tb_ref, tc_ref, h0_ref,
                y_ref, hout_ref, txs_scr, tb_scr, tc_scr, h_scr):
    ci = pl.program_id(2)

    @pl.when(ci == 0)
    def _():
        txs_scr[...] = txs_ref[...]
        tb_scr[...] = tb_ref[...]
        tc_scr[...] = tc_ref[...]
        h_scr[...] = h0_ref[...].reshape(h_scr.shape)

    for s in range(nb):
        _ssd_chunk(lb, s, xs_ref, b_ref, c_ref, dt_ref, wxs_ref, wb_ref, wc_ref, bxs_ref, bb_ref, bc_ref,
                   dtb_ref, alog_ref, dsk_ref, y_ref, txs_scr, tb_scr, tc_scr, h_scr)

    @pl.when(ci == pl.num_programs(2) - 1)
    def _():
        hout_ref[...] = h_scr[...].reshape(hout_ref.shape)


def _ssd_chunk(lb, s, xs_ref, b_ref, c_ref, dt_ref, wxs_ref, wb_ref, wc_ref, bxs_ref, bb_ref, bc_ref,
               dtb_ref, alog_ref, dsk_ref, y_ref, txs_scr, tb_scr, tc_scr, h_scr):
    L = max(lb, 2 * SUBLANES)
    g = pl.program_id(1)

    def pad_rows(x):
        if lb == L:
            return x
        return jnp.concatenate([x, jnp.zeros((L - lb, x.shape[1]), x.dtype)], axis=0)

    def conv_silu(x_ref, tail_scr, w_ref, bias_ref):
        x = pad_rows(x_ref[s].astype(F32))
        tail = tail_scr[s]
        w = w_ref[...]
        row = lax.broadcasted_iota(jnp.int32, (SUBLANES, x.shape[1]), 0)
        out = bias_ref[...] + w[CONV_WIDTH - 1:CONV_WIDTH] * x
        for k in range(1, CONV_WIDTH):
            sh = pltpu.roll(x, k, 0)
            head = jnp.where(row < k, pltpu.roll(tail, k, 0), sh[:SUBLANES])
            sh = jnp.concatenate([head, sh[SUBLANES:]], axis=0)
            out = out + w[CONV_WIDTH - 1 - k:CONV_WIDTH - k] * sh
        tail_scr[s] = x[lb - SUBLANES:lb]
        return _silu(out)

    xs = conv_silu(xs_ref, txs_scr, wxs_ref, bxs_ref)
    bm = conv_silu(b_ref, tb_scr, wb_ref, bb_ref)
    cm = conv_silu(c_ref, tc_scr, wc_ref, bc_ref)

    shift = (LANES - g * HEADS_PER_GROUP) % LANES
    dt_raw = pltpu.roll(pad_rows(dt_ref[s]) + dtb_ref[...], shift, 1)
    a = -jnp.exp(pltpu.roll(alog_ref[...], shift, 1))
    dt = jax.nn.softplus(dt_raw)
    if lb < L:
        dt = jnp.where(lax.broadcasted_iota(jnp.int32, dt.shape, 0) < lb, dt, 0.0)
    la = dt * a

    r_i = lax.broadcasted_iota(jnp.int32, (L, L), 0)
    c_i = lax.broadcasted_iota(jnp.int32, (L, L), 1)
    tril = r_i >= c_i
    cum = _split3_dot(tril.astype(BF16), la)
    cum_t = cum.T
    ecum = jnp.exp(cum)
    to_end = jnp.exp(cum[L - 1:L] - cum)

    e_r = lax.broadcasted_iota(jnp.int32, (LANES, GROUP_CH), 0)
    e_c = lax.broadcasted_iota(jnp.int32, (LANES, GROUP_CH), 1) // SSM_HEAD_DIM
    expand = (e_r == e_c).astype(BF16)
    dt_e, ecum_e, to_end_e = _split2_dot_stacked([dt, ecum, to_end], expand)

    xdt = xs * dt_e
    bmb, cmb = bm.astype(BF16), cm.astype(BF16)
    cb = lax.dot_general(cmb, bmb, _NT, preferred_element_type=F32)
    half = lax.broadcasted_iota(jnp.int32, (L, LANES), 1) // SSM_HEAD_DIM
    pairs = []
    for pr in range(HEADS_PER_GROUP // 2):
        xp = xdt[:, pr * LANES:(pr + 1) * LANES]
        acc = jnp.zeros((L, LANES), F32)
        for hh in range(2):
            h = 2 * pr + hh
            seg = cum[:, h:h + 1] - cum_t[h:h + 1, :]
            w = (cb * jnp.exp(jnp.where(tril, seg, -jnp.inf))).astype(BF16)
            acc += _dot(w, jnp.where(half == hh, xp, 0.0).astype(BF16))
        pairs.append(acc)
    y_diag = jnp.concatenate(pairs, axis=1)

    h_prev = h_scr[s]
    y_off = lax.dot_general(cmb, h_prev.astype(BF16), _NT, preferred_element_type=F32) * ecum_e
    y = y_diag + y_off + dsk_ref[...] * xs
    y_ref[s] = y[:lb].astype(y_ref.dtype)

    xw_t = (xdt * to_end_e).T.astype(BF16)
    upd = _dot(xw_t, bmb)
    d_last = jnp.exp(cum_t[:, L - 1:L])
    dec = jnp.concatenate([jnp.broadcast_to(d_last[h:h + 1, :], (SSM_HEAD_DIM, SSM_STATE))
                           for h in range(HEADS_PER_GROUP)], axis=0)
    h_scr[s] = h_prev * dec + upd


def _ssd(xbc, small, conv_w, conv_b, dt_bias, a_log, d_skip, conv_tail, h0, bsz, t):
    lb = min(t, SSM_CHUNK)
    nc = t // lb
    gs, hpg = SSM_GROUPS, HEADS_PER_GROUP
    b_blk0 = SSM_INNER // SSM_STATE
    c_blk0 = b_blk0 + gs
    pad128 = lambda v: jnp.pad(v, (0, LANES - v.shape[0])).reshape(1, LANES)
    dsk = jnp.repeat(d_skip, SSM_HEAD_DIM).reshape(1, SSM_INNER)
    cbias = conv_b.reshape(1, CONV_DIM)

    nb = _tile(bsz, 8)
    seq3 = lambda w, col: pl.BlockSpec((nb, lb, w), lambda b, g, c: (b, c, col(g)))
    par2 = lambda r, w, col: pl.BlockSpec((r, w), lambda b, g, c: (0, col(g)))
    tail3 = lambda w, col: pl.BlockSpec((nb, SUBLANES, w), lambda b, g, c: (b, 0, col(g)))
    xs_col = lambda g: g
    b_col = lambda g: b_blk0 + g
    c_col = lambda g: c_blk0 + g
    zero = lambda g: 0
    state_spec = pl.BlockSpec((nb, hpg, SSM_HEAD_DIM, SSM_STATE), lambda b, g, c: (b, g, 0, 0))
    y, h_out = pl.pallas_call(
        functools.partial(_ssd_kernel, lb, nb),
        grid=(bsz // nb, gs, nc),
        in_specs=[seq3(GROUP_CH, xs_col), seq3(SSM_STATE, b_col), seq3(SSM_STATE, c_col),
                  seq3(LANES, lambda g: DT_CHUNK),
                  par2(CONV_WIDTH, GROUP_CH, xs_col), par2(CONV_WIDTH, SSM_STATE, b_col),
                  par2(CONV_WIDTH, SSM_STATE, c_col),
                  par2(1, GROUP_CH, xs_col), par2(1, SSM_STATE, b_col), par2(1, SSM_STATE, c_col),
                  par2(1, LANES, zero), par2(1, LANES, zero), par2(1, GROUP_CH, xs_col),
                  tail3(GROUP_CH, xs_col), tail3(SSM_STATE, b_col), tail3(SSM_STATE, c_col),
                  state_spec],
        out_specs=[seq3(GROUP_CH, xs_col), state_spec],
        out_shape=[jax.ShapeDtypeStruct((bsz, t, SSM_INNER), BF16),
                   jax.ShapeDtypeStruct(h0.shape, F32)],
        scratch_shapes=[pltpu.VMEM((nb, SUBLANES, GROUP_CH), F32), pltpu.VMEM((nb, SUBLANES, SSM_STATE), F32),
                        pltpu.VMEM((nb, SUBLANES, SSM_STATE), F32),
                        pltpu.VMEM((nb, GROUP_CH, SSM_STATE), F32)],
        compiler_params=_params("parallel", "parallel", "arbitrary"),
    )(xbc, xbc, xbc, small, conv_w, conv_w, conv_w, cbias, cbias, cbias,
      pad128(dt_bias), pad128(a_log), dsk, conv_tail, conv_tail, conv_tail, h0)
    return y, h_out


def _mix_kernel(x_ref, attn_ref, y_ref, z_ref, gates_ref, gate_ref, gssm_ref, gpost_ref,
                woa_ref, wos_ref, wout_ref, o_ref):
    o_attn = _dot(attn_ref[...], woa_ref[...])
    yz = y_ref[...].astype(F32) * _silu(z_ref[...].astype(F32))
    o_ssm = _dot(_rms(yz, gssm_ref[...]).astype(BF16), wos_ref[...])
    gates = gates_ref[...].astype(F32)
    mixed = (jax.nn.sigmoid(gates[:, :D_MODEL]) * o_attn
             + jax.nn.sigmoid(gates[:, D_MODEL:]) * o_ssm)
    m2 = _dot(mixed.astype(BF16), wout_ref[...])
    o_ref[...] = x_ref[...] + gate_ref[...] * _rms(m2, gpost_ref[...])


def _mix(x, attn, y, zg, gate, g_ssm, g_post, woa, wos, wout, rows_per_seq):
    n, d = x.shape
    tm = _tile(rows_per_seq if gate.ndim == 3 else n, 256)
    rows = lambda w, col: pl.BlockSpec((tm, w), lambda i: (i, col))
    const = lambda r, c: pl.BlockSpec((r, c), lambda i: (0, 0))
    return pl.pallas_call(
        _mix_kernel,
        grid=(n // tm,),
        in_specs=[rows(d, 0), rows(attn.shape[1], 0), rows(SSM_INNER, 0), rows(SSM_INNER, 0),
                  rows(2 * d, 1), _mod_spec(gate, tm, rows_per_seq, 1),
                  const(1, SSM_INNER), const(1, d),
                  const(*woa.shape), const(*wos.shape), const(*wout.shape)],
        out_specs=rows(d, 0),
        out_shape=jax.ShapeDtypeStruct((n, d), F32),
        compiler_params=_params("parallel"),
    )(x, attn, y, zg, zg, gate, g_ssm.reshape(1, -1), g_post.reshape(1, -1), woa, wos, wout)


def _prep_weights(w_in, w_uq, w_ukv, w_o_attn):
    o = IN_OFFSETS
    seg = lambda i: w_in[:, o[i]:o[i + 1]]
    zeros = lambda c: jnp.zeros((D_MODEL, c), w_in.dtype)
    w_proj = jnp.concatenate([seg(0), seg(1), zeros(QK_NOPE), seg(2), zeros(LANES - QK_NOPE - QK_ROPE),
                              seg(5), zeros(LANES - SSM_HEADS), seg(4), seg(3), seg(6), seg(7)], axis=1)
    hd = QK_NOPE + QK_ROPE
    wuq = jnp.pad(w_uq.reshape(Q_LORA, MLA_HEADS, hd), ((0, 0), (0, 0), (0, HEAD_PAD - hd)))
    wuq = wuq.reshape(Q_LORA, MLA_HEADS * HEAD_PAD)
    wkv = w_ukv.reshape(KV_LORA, MLA_HEADS, QK_NOPE + V_HEAD)
    w_uk, w_uv = wkv[..., :QK_NOPE], wkv[..., QK_NOPE:]
    pad_h = lambda w, width: jnp.pad(w, ((0, 0), (0, 0), (0, HEAD_PAD - width)))
    wk = pad_h(w_uk, QK_NOPE).reshape(KV_LORA, MLA_HEADS * HEAD_PAD)
    wv = pad_h(w_uv, V_HEAD).reshape(KV_LORA, MLA_HEADS * HEAD_PAD)
    w_abs = jnp.zeros((MLA_HEADS, HEAD_PAD, KV_LORA + LANES), w_ukv.dtype)
    w_abs = w_abs.at[:, :QK_NOPE, :KV_LORA].set(jnp.transpose(w_uk, (1, 2, 0)))
    w_abs = w_abs.at[:, QK_NOPE:hd, KV_LORA:KV_LORA + QK_ROPE].set(jnp.eye(QK_ROPE, dtype=w_ukv.dtype))
    w_uv_h = pad_h(jnp.transpose(w_uv, (1, 0, 2)), V_HEAD)
    woa = jnp.pad(w_o_attn.reshape(MLA_HEADS, V_HEAD, D_MODEL), ((0, 0), (0, HEAD_PAD - V_HEAD), (0, 0)))
    woa = woa.reshape(MLA_HEADS * HEAD_PAD, D_MODEL)
    cast = lambda w: w.astype(BF16)
    return tuple(map(cast, (w_proj, wuq, wk, wv, w_abs, w_uv_h, woa)))


def _rope_tables(pos):
    half = QK_ROPE // 2
    inv = ROPE_THETA ** (-jnp.arange(half, dtype=F32) / half)
    ang = pos.astype(F32)[:, None] * inv[None, :]
    cos, sin = jnp.cos(ang), jnp.sin(ang)
    n = pos.shape[0]
    ones, zeros = jnp.ones((n, QK_NOPE), F32), jnp.zeros((n, QK_NOPE), F32)
    tail = jnp.zeros((n, LANES - QK_NOPE - QK_ROPE), F32)
    return (jnp.concatenate([ones, cos, cos, tail], axis=1),
            jnp.concatenate([zeros, -sin, sin, tail], axis=1))


def _layer(x, mods, pos, per_row, conv_prev, ssm_prev, paged, lw):
    (g_pre, g_post, wg, wu, wd, w_proj, g_q, wuq, g_kv, wk, wv, w_abs, w_uv_h, woa,
     conv_w, conv_b, dt_bias, a_log, d_skip, g_ssm, wos, wout) = lw
    bsz, t, d = x.shape
    n = bsz * t
    x2 = x.reshape(n, d)
    if per_row:
        mod = lambda s, k: jnp.repeat(mods[:, s, k], t, axis=0)
    else:
        mod = lambda s, k: mods[:, s, k][:, None, :]

    x2 = _ffn(x2, mod(0, 0), mod(0, 1), mod(0, 2), g_pre[0], g_post[0], wg[0], wu[0], wd[0], t)

    sh, sc = mod(1, 0), mod(1, 1)
    small, xbc, zg = _proj(x2, sh, sc, g_pre[1], w_proj, (SMALL_W, CONV_DIM, SSM_INNER + 2 * D_MODEL), t)

    cos, sin = _rope_tables(pos)
    if per_row:
        tm = _tile(n, 512)
        cos, sin = jnp.tile(cos, (tm // t, 1)), jnp.tile(sin, (tm // t, 1))
    mla = _mla(small, cos, sin, g_q, g_kv, wuq, wk, wv, t, paged is None)
    if paged is None:
        q, c_kv, k_pe, k, v = mla
        attn = _prompt_attention(q, k, v, bsz, t)
    else:
        q, c_kv, k_pe = mla
        cache_c, cache_pe, page_table = paged
        q_full = _head_mm(q, w_abs).reshape(bsz, t * MLA_HEADS, KV_LORA + LANES)
        o_lat = _paged_attention(q_full, c_kv.reshape(bsz, t, KV_LORA), k_pe.reshape(bsz, t, QK_ROPE),
                                 cache_c, cache_pe, page_table)
        attn = _head_mm(o_lat.reshape(n, MLA_HEADS * KV_LORA), w_uv_h)

    xbc3 = xbc.reshape(bsz, t, CONV_DIM)
    tail = jnp.pad(conv_prev.astype(F32), ((0, 0), (SUBLANES - (CONV_WIDTH - 1), 0), (0, 0)))
    y, new_ssm = _ssd(xbc3, small.reshape(bsz, t, SMALL_W), conv_w, conv_b, dt_bias, a_log, d_skip,
                      tail, ssm_prev, bsz, t)
    u_tail = jnp.concatenate([conv_prev.astype(F32), xbc3[:, -(CONV_WIDTH - 1):].astype(F32)], axis=1)
    new_conv = u_tail[:, -(CONV_WIDTH - 1):]

    if paged is None:
        woa = woa.reshape(MLA_HEADS, HEAD_PAD, D_MODEL)[:, :V_HEAD].reshape(MLA_HEADS * V_HEAD, D_MODEL)
    x2 = _mix(x2, attn, y.reshape(n, SSM_INNER), zg, mod(1, 2), g_ssm, g_post[1], woa, wos, wout, t)
    x2 = _ffn(x2, mod(2, 0), mod(2, 1), mod(2, 2), g_pre[2], g_post[2], wg[1], wu[1], wd[1], t)
    return (x2.reshape(bsz, t, d), c_kv.reshape(bsz, t, KV_LORA), k_pe.reshape(bsz, t, QK_ROPE),
            new_conv, new_ssm)


def kernel(x_prompt, x_sample, cache_kv_latent, cache_k_rope, state_conv, state_ssm, page_table,
           c_prompt, c_sample, w_ada, b_ada, g_pre, g_post, w_ffn_gate, w_ffn_up, w_ffn_down, w_in,
           g_q_lat, w_uq, g_kv_lat, w_ukv, w_o_attn, conv_w, conv_b, dt_bias, a_log, d_skip,
           g_ssm_norm, w_o_ssm, w_out):
    bp, t_prompt = x_prompt.shape[:2]
    bs, t_sample = x_sample.shape[:2]
    depth = w_in.shape[0]
    past_len = page_table.shape[1] * cache_kv_latent.shape[2]
    pos_p = jnp.arange(t_prompt, dtype=jnp.int32)
    pos_s = past_len + jnp.arange(t_sample, dtype=jnp.int32)
    yp, ys = x_prompt, x_sample
    outs = [[] for _ in range(8)]
    pad_c = (-(bp + bs)) % SUBLANES
    c_all = jnp.concatenate([c_prompt, c_sample, jnp.zeros((pad_c, D_MODEL), c_prompt.dtype)], axis=0)
    for l in range(depth):
        mods = _ada(c_all, w_ada[l], b_ada[l]).reshape(-1, N_SUB, 3, D_MODEL)
        prepped = _prep_weights(w_in[l], w_uq[l], w_ukv[l], w_o_attn[l])
        w_proj, wuq, wk, wv, w_abs, w_uv_h, woa = prepped
        cast = lambda w: w.astype(BF16)
        lw = (g_pre[l], g_post[l], cast(w_ffn_gate[l]), cast(w_ffn_up[l]), cast(w_ffn_down[l]),
              w_proj, g_q_lat[l], wuq, g_kv_lat[l], wk, wv, w_abs, w_uv_h, woa,
              conv_w[l], conv_b[l], dt_bias[l], a_log[l], d_skip[l], g_ssm_norm[l],
              cast(w_o_ssm[l]), cast(w_out[l]))
        conv0 = jnp.zeros((bp, CONV_WIDTH - 1, CONV_DIM), x_prompt.dtype)
        ssm0 = jnp.zeros((bp, SSM_HEADS, SSM_HEAD_DIM, SSM_STATE), state_ssm.dtype)
        res_p = _layer(yp, mods[:bp], pos_p, False, conv0, ssm0, None, lw)
        paged = (cache_kv_latent[l], jnp.swapaxes(cache_k_rope[l], 1, 2), page_table)
        res_s = _layer(ys, mods[bp:bp + bs], pos_s, True, state_conv[l], state_ssm[l], paged, lw)
        yp, ys = res_p[0], res_s[0]
        for i in range(4):
            outs[i].append(res_p[1 + i])
            outs[4 + i].append(res_s[1 + i])
    return (yp, ys) + tuple(jnp.stack(o) for o in outs)
```
